```python
import math
import jax, jax.numpy as jnp
from jax import lax
import numpy as np

D_MODEL = 2048
BATCH = 8
SEQ = 2048
DEPTH = 2
DEC_BATCH = 8
DEC_SEQ = 64
PAST_LEN = 2048

CHUNK = 64
N_MIXERS = 2
N_ATTN_LAYERS = (DEPTH + 1) // 2
N_POOL_LAYERS = DEPTH // 2
HEAD_DIM = 64
N_HEADS = D_MODEL // HEAD_DIM
N_KV_HEADS = N_HEADS // 8
GQA_GROUP = N_HEADS // N_KV_HEADS
Q_DIM = N_HEADS * HEAD_DIM
KV_DIM = N_KV_HEADS * HEAD_DIM
QKV_DIM = Q_DIM + 2 * KV_DIM
WINDOW = 128
N_WIN_CHUNKS = WINDOW // CHUNK
ROT_DIM = HEAD_DIM // 4
ROPE_THETA = 500000.0
POOL_WINDOWS = (2, 4, 8, 16)
N_POOL_GROUPS = len(POOL_WINDOWS)
POOL_GROUP_DIM = D_MODEL // N_POOL_GROUPS
POOL_HIST = max(POOL_WINDOWS) - 1
D_FF = 4 * D_MODEL
PLE_DIM = 256
EPS = 1e-6

kernel_name = "chunk_stream_swa_sink_pool_hybrid"


def rms_norm(x, g):
    xf = x.astype(jnp.float32)
    y = xf * lax.rsqrt(jnp.mean(xf * xf, axis=-1, keepdims=True) + EPS)
    return (y * g.astype(jnp.float32)).astype(x.dtype)


def rope_partial(x, pos):
    half = ROT_DIM // 2
    inv = ROPE_THETA ** (-jnp.arange(0, ROT_DIM, 2, dtype=jnp.float32) / ROT_DIM)
    ang = pos.astype(jnp.float32)[:, None] * inv[None, :]
    cos = jnp.cos(ang)[None, :, None, :]
    sin = jnp.sin(ang)[None, :, None, :]
    x1 = x[..., :half].astype(jnp.float32)
    x2 = x[..., half:ROT_DIM].astype(jnp.float32)
    r = jnp.concatenate([x1 * cos - x2 * sin, x2 * cos + x1 * sin], axis=-1).astype(x.dtype)
    return jnp.concatenate([r, x[..., ROT_DIM:]], axis=-1)


def qkv_proj(h, w_qkv, b_qkv, pos):
    B, T, _ = h.shape
    qkv = h @ w_qkv + b_qkv
    q = qkv[..., :Q_DIM].reshape(B, T, N_HEADS, HEAD_DIM)
    k = qkv[..., Q_DIM:Q_DIM + KV_DIM].reshape(B, T, N_KV_HEADS, HEAD_DIM)
    v = qkv[..., Q_DIM + KV_DIM:].reshape(B, T, N_KV_HEADS, HEAD_DIM)
    return rope_partial(q, pos), rope_partial(k, pos), v


def sink_softmax(s, sink):
    sk = sink.astype(jnp.float32).reshape(N_KV_HEADS, GQA_GROUP)[:, :, None, None]
    m = jnp.maximum(jnp.max(s, axis=-1, keepdims=True), sk)
    e = jnp.exp(s - m)
    return e / (jnp.sum(e, axis=-1, keepdims=True) + jnp.exp(sk - m))


def attn_prompt(h, w_qkv, b_qkv, w_o, sinks):
    B, S, _ = h.shape
    nc = S // CHUNK
    q, k, v = qkv_proj(h, w_qkv, b_qkv, jnp.arange(S, dtype=jnp.int32))
    pad = ((0, 0), (WINDOW, 0), (0, 0), (0, 0))
    kp = jnp.pad(k, pad).reshape(B, nc + N_WIN_CHUNKS, CHUNK, N_KV_HEADS, HEAD_DIM)
    vp = jnp.pad(v, pad).reshape(B, nc + N_WIN_CHUNKS, CHUNK, N_KV_HEADS, HEAD_DIM)
    kwin = jnp.concatenate([kp[:, j:j + nc] for j in range(N_WIN_CHUNKS + 1)], axis=2)
    vwin = jnp.concatenate([vp[:, j:j + nc] for j in range(N_WIN_CHUNKS + 1)], axis=2)
    L = WINDOW + CHUNK
    key_pos = jnp.arange(nc)[:, None] * CHUNK - WINDOW + jnp.arange(L)[None, :]
    valid = key_pos >= 0
    qb = q.reshape(B, nc, CHUNK, N_KV_HEADS, GQA_GROUP, HEAD_DIM)
    s = jnp.einsum('bnqkgd,bnlkd->bnkgql', qb, kwin).astype(jnp.float32) * (1.0 / math.sqrt(HEAD_DIM))
    s = jnp.where(valid[None, :, None, None, None, :], s, -jnp.inf)
    p = sink_softmax(s, sinks).astype(v.dtype)
    o = jnp.einsum('bnkgql,bnlkd->bnqkgd', p, vwin).reshape(B, S, Q_DIM)
    return o @ w_o, k[:, -WINDOW:], v[:, -WINDOW:]


def attn_sample(h, ck, cv, w_qkv, b_qkv, w_o, sinks):
    B, T, _ = h.shape
    pos = PAST_LEN + jnp.arange(T, dtype=jnp.int32)
    q, k, v = qkv_proj(h, w_qkv, b_qkv, pos)
    kall = jnp.concatenate([ck.astype(k.dtype), k], axis=1)
    vall = jnp.concatenate([cv.astype(v.dtype), v], axis=1)
    qb = q.reshape(B, T, N_KV_HEADS, GQA_GROUP, HEAD_DIM)
    s = jnp.einsum('btkgd,blkd->bkgtl', qb, kall).astype(jnp.float32) * (1.0 / math.sqrt(HEAD_DIM))
    p = sink_softmax(s, sinks).astype(v.dtype)
    o = jnp.einsum('bkgtl,blkd->btkgd', p, vall).reshape(B, T, Q_DIM)
    return o @ w_o, k, v


def pool_mix(h, hist, pos0, w_pool, scale):
    B, T, D = h.shape
    xx = jnp.concatenate([hist.astype(h.dtype), h], axis=1)
    csp = jnp.pad(jnp.cumsum(xx.astype(jnp.float32), axis=1), ((0, 0), (1, 0), (0, 0)))
    pos = pos0 + jnp.arange(T)
    base = POOL_HIST + 1
    means = []
    for g, w in enumerate(POOL_WINDOWS):
        sl = slice(g * POOL_GROUP_DIM, (g + 1) * POOL_GROUP_DIM)
        win_sum = csp[:, base:base + T, sl] - csp[:, base - w:base - w + T, sl]
        cnt = jnp.minimum(pos + 1, w).astype(jnp.float32)[None, :, None]
        means.append(win_sum / cnt)
    d = (jnp.concatenate(means, axis=-1) - h.astype(jnp.float32)).astype(h.dtype)
    d = d.reshape(B, T, N_POOL_GROUPS, POOL_GROUP_DIM)
    y = jnp.einsum('btgc,gcd->btgd', d, w_pool).reshape(B, T, D) * scale
    return y, xx[:, -POOL_HIST:]


def ffn_and_ple(x, p, g_pre, g_post, w_up, w_down, w_ple_proj, w_ple_gate):
    h = rms_norm(x, g_pre)
    u = jax.nn.relu(h @ w_up)
    x = x + rms_norm((u * u) @ w_down, g_post)
    gate = jax.nn.sigmoid((x @ w_ple_gate).astype(jnp.float32)).astype(x.dtype)
    return x + gate * (p @ w_ple_proj)


def setup_inputs(seed: int = 0) -> dict:
    key = jax.random.key(seed)
    ks = jax.random.split(key, 24)
    f32 = jnp.float32
    nrm = lambda k, shape, s: jax.random.normal(k, shape, f32) * s
    return {
        "x_prompt": nrm(ks[0], (BATCH, SEQ, D_MODEL), 1.0),
        "x_sample": nrm(ks[1], (DEC_BATCH, DEC_SEQ, D_MODEL), 1.0),
        "cache_k": nrm(ks[2], (N_ATTN_LAYERS, DEC_BATCH, WINDOW, N_KV_HEADS, HEAD_DIM), 1.0),
        "cache_v": nrm(ks[3], (N_ATTN_LAYERS, DEC_BATCH, WINDOW, N_KV_HEADS, HEAD_DIM), 1.0),
        "state_pool": nrm(ks[4], (N_POOL_LAYERS, DEC_BATCH, POOL_HIST, D_MODEL), 1.0),
        "p_prompt": nrm(ks[5], (DEPTH, BATCH, SEQ, PLE_DIM), 1.0),
        "p_sample": nrm(ks[6], (DEPTH, DEC_BATCH, DEC_SEQ, PLE_DIM), 1.0),
        "norm_mix_pre": 1.0 + nrm(ks[7], (DEPTH, D_MODEL), 0.05),
        "norm_mix_post": 1.0 + nrm(ks[8], (DEPTH, D_MODEL), 0.05),
        "norm_ffn_pre": 1.0 + nrm(ks[9], (DEPTH, D_MODEL), 0.05),
        "norm_ffn_post": 1.0 + nrm(ks[10], (DEPTH, D_MODEL), 0.05),
        "w_qkv": nrm(ks[11], (N_ATTN_LAYERS, D_MODEL, QKV_DIM), D_MODEL ** -0.5),
        "b_qkv": nrm(ks[12], (N_ATTN_LAYERS, QKV_DIM), 0.02),
        "w_o": nrm(ks[13], (N_ATTN_LAYERS, Q_DIM, D_MODEL), Q_DIM ** -0.5),
        "sinks": nrm(ks[14], (N_ATTN_LAYERS, N_HEADS), 0.5),
        "w_pool": nrm(ks[15], (N_POOL_LAYERS, N_POOL_GROUPS, POOL_GROUP_DIM, POOL_GROUP_DIM), POOL_GROUP_DIM ** -0.5),
        "pool_scale": 1.0 + nrm(ks[16], (N_POOL_LAYERS, D_MODEL), 0.1),
        "w_ffn_up": nrm(ks[17], (DEPTH, D_MODEL, D_FF), D_MODEL ** -0.5),
        "w_ffn_down": nrm(ks[18], (DEPTH, D_FF, D_MODEL), D_FF ** -0.5),
        "w_ple_proj": nrm(ks[19], (DEPTH, PLE_DIM, D_MODEL), PLE_DIM ** -0.5),
        "w_ple_gate": nrm(ks[20], (DEPTH, D_MODEL, D_MODEL), D_MODEL ** -0.5),
    }


def reference(x_prompt, x_sample, cache_k, cache_v, state_pool, p_prompt, p_sample,
              norm_mix_pre, norm_mix_post, norm_ffn_pre, norm_ffn_post,
              w_qkv, b_qkv, w_o, sinks, w_pool, pool_scale,
              w_ffn_up, w_ffn_down, w_ple_proj, w_ple_gate):
    xp, xs = x_prompt, x_sample
    kp_list, vp_list, sp_list = [], [], []
    ks_list, vs_list, ss_list = [], [], []
    for i in range(DEPTH):
        hp = rms_norm(xp, norm_mix_pre[i])
        hs = rms_norm(xs, norm_mix_pre[i])
        j = i // N_MIXERS
        if i % N_MIXERS == 0:
            mp, k_p, v_p = attn_prompt(hp, w_qkv[j], b_qkv[j], w_o[j], sinks[j])
            ms, k_s, v_s = attn_sample(hs, cache_k[j], cache_v[j], w_qkv[j], b_qkv[j], w_o[j], sinks[j])
            kp_list.append(k_p); vp_list.append(v_p)
            ks_list.append(k_s); vs_list.append(v_s)
        else:
            zero_hist = jnp.zeros((hp.shape[0], POOL_HIST, D_MODEL), hp.dtype)
            mp, s_p = pool_mix(hp, zero_hist, 0, w_pool[j], pool_scale[j])
            ms, s_s = pool_mix(hs, state_pool[j], PAST_LEN, w_pool[j], pool_scale[j])
            sp_list.append(s_p); ss_list.append(s_s)
        xp = xp + rms_norm(mp, norm_mix_post[i])
        xs = xs + rms_norm(ms, norm_mix_post[i])
        xp = ffn_and_ple(xp, p_prompt[i], norm_ffn_pre[i], norm_ffn_post[i], w_ffn_up[i], w_ffn_down[i], w_ple_proj[i], w_ple_gate[i])
        xs = ffn_and_ple(xs, p_sample[i], norm_ffn_pre[i], norm_ffn_post[i], w_ffn_up[i], w_ffn_down[i], w_ple_proj[i], w_ple_gate[i])
    new_k_prompt = jnp.stack(kp_list)
    new_v_prompt = jnp.stack(vp_list)
    new_pool_prompt = jnp.stack(sp_list)
    new_k_sample = jnp.stack(ks_list)
    new_v_sample = jnp.stack(vs_list)
    new_pool_sample = jnp.stack(ss_list)
    return (xp, xs, new_k_prompt, new_v_prompt, new_pool_prompt, new_k_sample, new_v_sample, new_pool_sample)
```

```python
import functools
import math

import jax
import jax.numpy as jnp
from jax import lax
from jax.experimental import pallas as pl
from jax.experimental.pallas import tpu as pltpu

HEAD_DIM = 64
GQA_GROUP = 8
CHUNK = 64
WINDOW = 128
ROT_DIM = HEAD_DIM // 4
ROPE_THETA = 500000.0
POOL_WINDOWS = (2, 4, 8, 16)
POOL_HIST = max(POOL_WINDOWS) - 1
PAST_LEN = 2048
EPS = 1e-6

LANES = 128
HALO = 16
VMEM_LIMIT_BYTES = 60 * 1024 * 1024

F32 = jnp.float32
BF16 = jnp.bfloat16


def _rms(x, g):
    ms = jnp.mean(x * x, axis=-1, keepdims=True)
    return x * lax.rsqrt(ms + EPS) * g


def _resident(shape):
    nd = len(shape)
    return pl.BlockSpec(shape, lambda *_: (0,) * nd, pipeline_mode=pl.Buffered(1))


def _params(*sem):
    return pltpu.CompilerParams(dimension_semantics=sem, vmem_limit_bytes=VMEM_LIMIT_BYTES)


def _rope_tables(positions):
    half = ROT_DIM // 2
    inv = ROPE_THETA ** (-jnp.arange(0, ROT_DIM, 2, dtype=F32) / ROT_DIM)
    ang = positions.astype(F32)[:, None] * inv[None, :]
    cos, sin = jnp.cos(ang), jnp.sin(ang)
    p = positions.shape[0]
    ones = jnp.ones((p, HEAD_DIM - ROT_DIM), F32)
    zeros = jnp.zeros((p, HEAD_DIM - ROT_DIM), F32)
    zh = jnp.zeros((p, half), F32)
    c = jnp.concatenate([cos, cos, ones], axis=1)
    sa = jnp.concatenate([-sin, zh, zeros], axis=1)
    sb = jnp.concatenate([zh, sin, zeros], axis=1)
    rep = LANES // HEAD_DIM
    return tuple(jnp.tile(t, (1, rep)) for t in (c, sa, sb))


def _qkv_kernel(x_ref, g_ref, w_ref, b_ref, c_ref, sa_ref, sb_ref, q_ref, k_ref, v_ref, *, q_dim, kv_dim, nchunk):
    h = _rms(x_ref[...], g_ref[...]).astype(BF16)
    c, sa, sb = c_ref[...], sa_ref[...], sb_ref[...]
    half = ROT_DIM // 2
    rot_cols = q_dim + kv_dim
    total = q_dim + 2 * kv_dim
    for n0 in range(0, total, nchunk):
        acc = jnp.dot(h, w_ref[:, n0:n0 + nchunk], preferred_element_type=F32) + b_ref[:, n0:n0 + nchunk]
        for j0 in range(0, nchunk, LANES):
            col = n0 + j0
            blk = acc[:, j0:j0 + LANES]
            if col < rot_cols:
                blk = blk * c + pltpu.roll(blk, LANES - half, 1) * sa + pltpu.roll(blk, half, 1) * sb
            if col < q_dim:
                q_ref[:, col:col + LANES] = blk.astype(BF16)
            elif col < rot_cols:
                k_ref[:, col - q_dim:col - q_dim + LANES] = blk
            else:
                v_ref[:, col - rot_cols:col - rot_cols + LANES] = blk


def _qkv_rope(x, g, w_bf16, b, tables, *, tm, q_dim, kv_dim):
    m, d = x.shape
    total = q_dim + 2 * kv_dim
    ptiles = tables[0].shape[0] // tm
    tab_spec = pl.BlockSpec((tm, LANES), lambda i: (i % ptiles, 0))
    kern = functools.partial(_qkv_kernel, q_dim=q_dim, kv_dim=kv_dim, nchunk=256)
    return pl.pallas_call(
        kern,
        grid=(m // tm,),
        in_specs=[
            pl.BlockSpec((tm, d), lambda i: (i, 0)),
            _resident((1, d)),
            _resident((d, total)),
            _resident((1, total)),
            tab_spec, tab_spec, tab_spec,
        ],
        out_specs=[
            pl.BlockSpec((tm, q_dim), lambda i: (i, 0)),
            pl.BlockSpec((tm, kv_dim), lambda i: (i, 0)),
            pl.BlockSpec((tm, kv_dim), lambda i: (i, 0)),
        ],
        out_shape=[
            jax.ShapeDtypeStruct((m, q_dim), BF16),
            jax.ShapeDtypeStruct((m, kv_dim), F32),
            jax.ShapeDtypeStruct((m, kv_dim), F32),
        ],
        compiler_params=_params("parallel"),
        name="qkv_rope",
    )(x, g, w_bf16, b, *tables)


def _attn_kernel(sink_ref, q_ref, k_ref, v_ref, o_ref, *, n_kv, mask_history):
    c = pl.program_id(1)
    span = WINDOW + CHUNK
    start = pl.multiple_of(c * CHUNK, CHUNK)
    kw = k_ref[pl.ds(start, span), :]
    vw = v_ref[pl.ds(start, span), :]
    q = q_ref[...]
    scale = 1.0 / math.sqrt(HEAD_DIM)
    if mask_history:
        key_pos = c * CHUNK - WINDOW + lax.broadcasted_iota(jnp.int32, (1, span), 1)
        valid = key_pos >= 0
    for kh in range(n_kv):
        k_h = kw[:, kh * HEAD_DIM:(kh + 1) * HEAD_DIM]
        v_h = vw[:, kh * HEAD_DIM:(kh + 1) * HEAD_DIM]
        for g in range(GQA_GROUP):
            hd = kh * GQA_GROUP + g
            q_h = q[:, hd * HEAD_DIM:(hd + 1) * HEAD_DIM]
            s = lax.dot_general(q_h, k_h, (((1,), (1,)), ((), ())), preferred_element_type=F32) * scale
            if mask_history:
                s = jnp.where(valid, s, -jnp.inf)
            sk = sink_ref[hd]
            m = jnp.maximum(jnp.max(s, axis=-1, keepdims=True), sk)
            e = jnp.exp(s - m)
            denom = jnp.sum(e, axis=-1, keepdims=True) + jnp.exp(sk - m)
            p = (e / denom).astype(BF16)
            o_h = jnp.dot(p, v_h, preferred_element_type=F32)
            o_ref[:, hd * HEAD_DIM:(hd + 1) * HEAD_DIM] = o_h.astype(BF16)


def _attention(q, k_hist, v_hist, sinks, *, mask_history):
    b, s, q_dim = q.shape
    kv_dim = k_hist.shape[-1]
    rows = k_hist.shape[1]
    kern = functools.partial(_attn_kernel, n_kv=kv_dim // HEAD_DIM, mask_history=mask_history)
    return pl.pallas_call(
        kern,
        grid=(b, s // CHUNK),
        in_specs=[
            pl.BlockSpec(memory_space=pltpu.SMEM),
            pl.BlockSpec((None, CHUNK, q_dim), lambda i, c: (i, c, 0)),
            pl.BlockSpec((None, rows, kv_dim), lambda i, c: (i, 0, 0)),
            pl.BlockSpec((None, rows, kv_dim), lambda i, c: (i, 0, 0)),
        ],
        out_specs=pl.BlockSpec((None, CHUNK, q_dim), lambda i, c: (i, c, 0)),
        out_shape=jax.ShapeDtypeStruct((b, s, q_dim), BF16),
        compiler_params=_params("parallel", "arbitrary"),
        name="swa_attention",
    )(sinks, q, k_hist, v_hist)


def _oproj_kernel(o_ref, x_ref, w_ref, g_ref, y_ref):
    mix = jnp.dot(o_ref[...], w_ref[...], preferred_element_type=F32)
    y_ref[...] = x_ref[...] + _rms(mix, g_ref[...])


def _out_proj(o, x, w_bf16, g, *, tm):
    m, d = x.shape
    return pl.pallas_call(
        _oproj_kernel,
        grid=(m // tm,),
        in_specs=[
            pl.BlockSpec((tm, o.shape[1]), lambda i: (i, 0)),
            pl.BlockSpec((tm, d), lambda i: (i, 0)),
            _resident(w_bf16.shape),
            _resident((1, d)),
        ],
        out_specs=pl.BlockSpec((tm, d), lambda i: (i, 0)),
        out_shape=jax.ShapeDtypeStruct((m, d), F32),
        compiler_params=_params("parallel"),
        name="attn_out_proj",
    )(o, x, w_bf16, g)


def _pool_kernel(x_ref, prev_ref, gpre_ref, gpost_ref, w_ref, scale_ref, y_ref, tail_ref,
                 *, tm, pos0, prev_is_normed, group_dim):
    i = pl.program_id(1)
    gpre = gpre_ref[...]
    x = x_ref[...]
    h = _rms(x, gpre)
    tail_ref[...] = h[tm - HALO:, :]
    if prev_is_normed:
        prev = prev_ref[...]
    else:
        prev = jnp.where(i > 0, _rms(prev_ref[...], gpre), 0.0)
    pos = pos0 + i * tm + lax.broadcasted_iota(jnp.int32, (tm, 1), 0)
    outs = []
    for gi, w in enumerate(POOL_WINDOWS):
        sl = slice(gi * group_dim, (gi + 1) * group_dim)
        hg = h[:, sl]
        acc = jnp.concatenate([prev[:, sl], hg], axis=0)
        shift = 1
        while shift < w:
            acc = acc + pltpu.roll(acc, shift, 0)
            shift *= 2
        cnt = jnp.minimum(pos + 1, w).astype(F32)
        dgrp = (acc[HALO:, :] / cnt - hg).astype(BF16)
        outs.append(jnp.dot(dgrp, w_ref[gi], preferred_element_type=F32))
    mix = jnp.concatenate(outs, axis=1) * scale_ref[...]
    y_ref[...] = x + _rms(mix, gpost_ref[...])


def _pool_mixer(x, prev, gpre, gpost, w_bf16, scale, *, tm, pos0, prev_is_normed):
    b, s, d = x.shape
    nt = s // tm
    group_dim = d // len(POOL_WINDOWS)
    if prev_is_normed:
        prev_spec = pl.BlockSpec((None, HALO, d), lambda bi, i: (bi, 0, 0))
    else:
        per = tm // HALO
        prev_spec = pl.BlockSpec((None, HALO, d), lambda bi, i: (bi, jnp.maximum(i * per - 1, 0), 0))
    kern = functools.partial(_pool_kernel, tm=tm, pos0=pos0, prev_is_normed=prev_is_normed, group_dim=group_dim)
    return pl.pallas_call(
        kern,
        grid=(b, nt),
        in_specs=[
            pl.BlockSpec((None, tm, d), lambda bi, i: (bi, i, 0)),
            prev_spec,
            _resident((1, d)),
            _resident((1, d)),
            _resident(w_bf16.shape),
            _resident((1, d)),
        ],
        out_specs=[
            pl.BlockSpec((None, tm, d), lambda bi, i: (bi, i, 0)),
            pl.BlockSpec((None, None, HALO, d), lambda bi, i: (bi, i, 0, 0)),
        ],
        out_shape=[
            jax.ShapeDtypeStruct((b, s, d), F32),
            jax.ShapeDtypeStruct((b, nt, HALO, d), F32),
        ],
        compiler_params=_params("parallel", "parallel"),
        name="pool_mixer",
    )(x, prev, gpre, gpost, w_bf16, scale)


def _ffn_kernel(x_ref, p_ref, gpre_ref, gpost_ref, wup_ref, wdown_ref, wgate_ref, wproj_ref, y_ref,
                h_scr, acc_scr, *, nf):
    f = pl.program_id(1)

    @pl.when(f == 0)
    def _():
        h_scr[...] = _rms(x_ref[...], gpre_ref[...]).astype(BF16)
        acc_scr[...] = jnp.zeros_like(acc_scr)

    u = jnp.maximum(jnp.dot(h_scr[...], wup_ref[...], preferred_element_type=F32), 0.0)
    acc_scr[...] += jnp.dot((u * u).astype(BF16), wdown_ref[...], preferred_element_type=F32)

    @pl.when(f == nf - 1)
    def _():
        x1 = x_ref[...] + _rms(acc_scr[...], gpost_ref[...])
        gate = jax.nn.sigmoid(jnp.dot(x1.astype(BF16), wgate_ref[...], preferred_element_type=F32))
        emb = jnp.dot(p_ref[...].astype(BF16), wproj_ref[...], preferred_element_type=F32)
        y_ref[...] = x1 + gate * emb


def _ffn_ple(x, p, gpre, gpost, wup, wdown, wgate, wproj, *, tm, tf):
    m, d = x.shape
    dff = wup.shape[1]
    nf = dff // tf
    return pl.pallas_call(
        functools.partial(_ffn_kernel, nf=nf),
        grid=(m // tm, nf),
        in_specs=[
            pl.BlockSpec((tm, d), lambda i, f: (i, 0)),
            pl.BlockSpec((tm, p.shape[1]), lambda i, f: (i, 0)),
            _resident((1, d)),
            _resident((1, d)),
            pl.BlockSpec((d, tf), lambda i, f: (0, f)),
            pl.BlockSpec((tf, d), lambda i, f: (f, 0)),
            _resident(wgate.shape),
            _resident(wproj.shape),
        ],
        out_specs=pl.BlockSpec((tm, d), lambda i, f: (i, 0)),
        out_shape=jax.ShapeDtypeStruct((m, d), F32),
        scratch_shapes=[pltpu.VMEM((tm, d), BF16), pltpu.VMEM((tm, d), F32)],
        compiler_params=_params("parallel", "arbitrary"),
        name="ffn_ple",
    )(x, p, gpre, gpost, wup, wdown, wgate, wproj)


def _row(v):
    return v.reshape(1, -1)


def kernel(x_prompt, x_sample, cache_k, cache_v, state_pool, p_prompt, p_sample, norm_mix_pre, norm_mix_post, norm_ffn_pre, norm_ffn_post, w_qkv, b_qkv, w_o, sinks, w_pool, pool_scale, w_ffn_up, w_ffn_down, w_ple_proj, w_ple_gate):
    b, s, d = x_prompt.shape
    bs, ts, _ = x_sample.shape
    past_len = PAST_LEN
    q_dim = w_o.shape[1]
    kv_dim = (w_qkv.shape[2] - q_dim) // 2
    n_kv = kv_dim // HEAD_DIM
    tm = 512
    assert s % tm == 0 and (bs * ts) % tm == 0 and tm % ts == 0

    def bf(w):
        return w.astype(BF16)

    xp = x_prompt.reshape(b * s, d)
    xs = x_sample.reshape(bs * ts, d)

    tab_p = _rope_tables(jnp.arange(s, dtype=jnp.int32))
    tab_s = tuple(jnp.tile(t, (tm // ts, 1)) for t in _rope_tables(past_len + jnp.arange(ts, dtype=jnp.int32)))
    wqkv, bqkv, wo = bf(w_qkv[0]), _row(b_qkv[0]), bf(w_o[0])
    g_pre, g_post = _row(norm_mix_pre[0]), _row(norm_mix_post[0])

    q_p, k_p, v_p = _qkv_rope(xp, g_pre, wqkv, bqkv, tab_p, tm=tm, q_dim=q_dim, kv_dim=kv_dim)
    q_s, k_s, v_s = _qkv_rope(xs, g_pre, wqkv, bqkv, tab_s, tm=tm, q_dim=q_dim, kv_dim=kv_dim)

    k_p3, v_p3 = k_p.reshape(b, s, kv_dim), v_p.reshape(b, s, kv_dim)
    k_s3, v_s3 = k_s.reshape(bs, ts, kv_dim), v_s.reshape(bs, ts, kv_dim)
    pad = ((0, 0), (WINDOW, 0), (0, 0))
    o_p = _attention(q_p.reshape(b, s, q_dim), jnp.pad(bf(k_p3), pad), jnp.pad(bf(v_p3), pad), sinks[0],
                     mask_history=True)
    ck = bf(jnp.concatenate([cache_k[0].reshape(bs, WINDOW, kv_dim), k_s3], axis=1))
    cv = bf(jnp.concatenate([cache_v[0].reshape(bs, WINDOW, kv_dim), v_s3], axis=1))
    o_s = _attention(q_s.reshape(bs, ts, q_dim), ck, cv, sinks[0], mask_history=False)

    xp = _out_proj(o_p.reshape(b * s, q_dim), xp, wo, g_post, tm=tm)
    xs = _out_proj(o_s.reshape(bs * ts, q_dim), xs, wo, g_post, tm=tm)

    def ffn(x, p, i):
        return _ffn_ple(x, p.reshape(-1, p.shape[-1]), _row(norm_ffn_pre[i]), _row(norm_ffn_post[i]),
                        bf(w_ffn_up[i]), bf(w_ffn_down[i]), bf(w_ple_gate[i]), bf(w_ple_proj[i]), tm=tm, tf=512)

    xp = ffn(xp, p_prompt[0], 0)
    xs = ffn(xs, p_sample[0], 0)

    g_pre, g_post = _row(norm_mix_pre[1]), _row(norm_mix_post[1])
    wpool, pscale = bf(w_pool[0]), _row(pool_scale[0])
    xp3, tail_p = _pool_mixer(xp.reshape(b, s, d), xp.reshape(b, s, d), g_pre, g_post, wpool, pscale,
                              tm=256, pos0=0, prev_is_normed=False)
    hist = jnp.pad(state_pool[0], ((0, 0), (HALO - POOL_HIST, 0), (0, 0)))
    xs3, tail_s = _pool_mixer(xs.reshape(bs, ts, d), hist, g_pre, g_post, wpool, pscale,
                              tm=ts, pos0=past_len, prev_is_normed=True)

    xp = ffn(xp3.reshape(b * s, d), p_prompt[1], 1)
    xs = ffn(xs3.reshape(bs * ts, d), p_sample[1], 1)

    n_heads_kv = (n_kv, HEAD_DIM)
    return (
        xp.reshape(b, s, d),
        xs.reshape(bs, ts, d),
        k_p3[:, s - WINDOW:].reshape(1, b, WINDOW, *n_heads_kv),
        v_p3[:, s - WINDOW:].reshape(1, b, WINDOW, *n_heads_kv),
        tail_p[:, -1, HALO - POOL_HIST:][None],
        k_s3.reshape(1, bs, ts, *n_heads_kv),
        v_s3.reshape(1, bs, ts, *n_heads_kv),
        tail_s[:, -1, HALO - POOL_HIST:][None],
    )
```

```python
import functools
import math

import jax
import jax.numpy as jnp
from jax import lax
from jax.experimental import pallas as pl
from jax.experimental.pallas import tpu as pltpu

HEAD_DIM = 64
GQA_GROUP = 8
CHUNK = 64
WINDOW = 128
ROT_DIM = HEAD_DIM // 4
ROPE_THETA = 500000.0
POOL_WINDOWS = (2, 4, 8, 16)
POOL_HIST = max(POOL_WINDOWS) - 1
PAST_LEN = 2048
EPS = 1e-6

LANES = 128
HALO = 16
VMEM_LIMIT_BYTES = 60 * 1024 * 1024

F32 = jnp.float32
BF16 = jnp.bfloat16


def _rms(x, g):
    ms = jnp.mean(x * x, axis=-1, keepdims=True)
    return x * lax.rsqrt(ms + EPS) * g


def _resident(shape, layer=None):
    if layer is None:
        nd = len(shape)
        return pl.BlockSpec(shape, lambda *_: (0,) * nd, pipeline_mode=pl.Buffered(1))
    nd = len(shape) - 1
    return pl.BlockSpec((None,) + tuple(shape[1:]), lambda *_: (layer,) + (0,) * nd, pipeline_mode=pl.Buffered(1))


def _params(*sem):
    return pltpu.CompilerParams(dimension_semantics=sem, vmem_limit_bytes=VMEM_LIMIT_BYTES)


def _rope_tables(positions):
    half = ROT_DIM // 2
    inv = ROPE_THETA ** (-jnp.arange(0, ROT_DIM, 2, dtype=F32) / ROT_DIM)
    ang = positions.astype(F32)[:, None] * inv[None, :]
    cos, sin = jnp.cos(ang), jnp.sin(ang)
    p = positions.shape[0]
    ones = jnp.ones((p, HEAD_DIM - ROT_DIM), F32)
    zeros = jnp.zeros((p, HEAD_DIM - ROT_DIM), F32)
    zh = jnp.zeros((p, half), F32)
    c = jnp.concatenate([cos, cos, ones], axis=1)
    sa = jnp.concatenate([-sin, zh, zeros], axis=1)
    sb = jnp.concatenate([zh, sin, zeros], axis=1)
    rep = LANES // HEAD_DIM
    return tuple(jnp.tile(t, (1, rep)) for t in (c, sa, sb))


def _qkv_kernel(x_ref, g_ref, w_ref, b_ref, c_ref, sa_ref, sb_ref, q_ref, k_ref, v_ref, *, q_dim, kv_dim, nchunk):
    h = _rms(x_ref[...], g_ref[...]).astype(BF16)
    c, sa, sb = c_ref[...], sa_ref[...], sb_ref[...]
    scale = 1.0 / math.sqrt(HEAD_DIM)
    cq, saq, sbq = c * scale, sa * scale, sb * scale
    half = ROT_DIM // 2
    rot_cols = q_dim + kv_dim
    total = q_dim + 2 * kv_dim
    for n0 in range(0, total, nchunk):
        acc = jnp.dot(h, w_ref[:, n0:n0 + nchunk], preferred_element_type=F32) + b_ref[:, n0:n0 + nchunk]
        for j0 in range(0, nchunk, LANES):
            col = n0 + j0
            blk = acc[:, j0:j0 + LANES]
            if col < q_dim:
                blk = blk * cq + pltpu.roll(blk, LANES - half, 1) * saq + pltpu.roll(blk, half, 1) * sbq
                q_ref[:, col:col + LANES] = blk.astype(BF16)
            elif col < rot_cols:
                blk = blk * c + pltpu.roll(blk, LANES - half, 1) * sa + pltpu.roll(blk, half, 1) * sb
                k_ref[:, col - q_dim:col - q_dim + LANES] = blk
            else:
                v_ref[:, col - rot_cols:col - rot_cols + LANES] = blk


def _qkv_rope(x, g, w_bf16, b, tables, *, tm, q_dim, kv_dim):
    m, d = x.shape
    total = q_dim + 2 * kv_dim
    ptiles = tables[0].shape[0] // tm
    tab_spec = pl.BlockSpec((tm, LANES), lambda i: (i % ptiles, 0))
    kern = functools.partial(_qkv_kernel, q_dim=q_dim, kv_dim=kv_dim, nchunk=256)
    return pl.pallas_call(
        kern,
        grid=(m // tm,),
        in_specs=[
            pl.BlockSpec((tm, d), lambda i: (i, 0)),
            _resident((1, d)),
            _resident((d, total)),
            _resident((1, total)),
            tab_spec, tab_spec, tab_spec,
        ],
        out_specs=[
            pl.BlockSpec((tm, q_dim), lambda i: (i, 0)),
            pl.BlockSpec((tm, kv_dim), lambda i: (i, 0)),
            pl.BlockSpec((tm, kv_dim), lambda i: (i, 0)),
        ],
        out_shape=[
            jax.ShapeDtypeStruct((m, q_dim), BF16),
            jax.ShapeDtypeStruct((m, kv_dim), F32),
            jax.ShapeDtypeStruct((m, kv_dim), F32),
        ],
        compiler_params=_params("parallel"),
        name="qkv_rope",
    )(x, g, w_bf16, b, *tables)


def _attn_kernel(sink_ref, q_ref, ka_ref, kb_ref, va_ref, vb_ref, o_ref, s_scr, *, n_kv, chunks, mask_history):
    step = pl.program_id(1)
    span = WINDOW + CHUNK
    pairs = GQA_GROUP // 2
    lane = lax.broadcasted_iota(jnp.int32, (span, LANES), 1)
    ones_a = jnp.where(lane < HEAD_DIM, 1.0, 0.0).astype(BF16)
    ones_b = jnp.where(lane >= HEAD_DIM, 1.0, 0.0).astype(BF16)
    is_a = lax.broadcasted_iota(jnp.int32, (pairs * CHUNK, LANES), 1) < HEAD_DIM
    nt = (((1,), (1,)), ((), ()))

    def chunk_body(ci, carry):
        row0 = pl.multiple_of(ci * CHUNK, CHUNK)
        cglob = step * chunks + ci
        k0 = pl.multiple_of(cglob * CHUNK, CHUNK)
        rows = pl.ds(row0, CHUNK)
        win = pl.ds(k0, span)

        for kh in range(n_kv):
            kcols = slice(kh * LANES, (kh + 1) * LANES)
            q_pairs = jnp.concatenate(
                [q_ref[rows, (kh * pairs + p) * LANES:(kh * pairs + p + 1) * LANES] for p in range(pairs)], axis=0)
            s_scr[kh, 0] = lax.dot_general(q_pairs, ka_ref[win, kcols], nt, preferred_element_type=F32)
            s_scr[kh, 1] = lax.dot_general(q_pairs, kb_ref[win, kcols], nt, preferred_element_type=F32)

        if mask_history:
            @pl.when(cglob * CHUNK < WINDOW)
            def _():
                key_pos = cglob * CHUNK - WINDOW + lax.broadcasted_iota(jnp.int32, (1, span), 1)
                bias = jnp.where(key_pos >= 0, 0.0, -jnp.inf)
                for kh in range(n_kv):
                    for ab in range(2):
                        s_scr[kh, ab] = s_scr[kh, ab] + bias

        e_all, t_all = [], []
        for kh in range(n_kv):
            for ab in range(2):
                e_rows, t_rows = [], []
                for p in range(pairs):
                    sk = sink_ref[kh * GQA_GROUP + 2 * p + ab]
                    s = s_scr[kh, ab, p * CHUNK:(p + 1) * CHUNK, :]
                    m = jnp.maximum(jnp.max(s, axis=-1, keepdims=True), sk)
                    e_rows.append(jnp.exp(s - m).astype(BF16))
                    t_rows.append(jnp.exp(sk - m))
                e_all.append(jnp.concatenate(e_rows, axis=0))
                t_all.append(jnp.concatenate(t_rows, axis=0))

        for kh in range(n_kv):
            kcols = slice(kh * LANES, (kh + 1) * LANES)
            rhs_a = jnp.concatenate([va_ref[win, kcols], ones_a], axis=1)
            rhs_b = jnp.concatenate([vb_ref[win, kcols], ones_b], axis=1)
            acc = (jnp.dot(e_all[2 * kh], rhs_a, preferred_element_type=F32)
                   + jnp.dot(e_all[2 * kh + 1], rhs_b, preferred_element_type=F32))
            den = acc[:, LANES:] + jnp.where(is_a, t_all[2 * kh], t_all[2 * kh + 1])
            o_pairs = (acc[:, :LANES] / den).astype(BF16)
            for p in range(pairs):
                col = (kh * pairs + p) * LANES
                o_ref[rows, col:col + LANES] = o_pairs[p * CHUNK:(p + 1) * CHUNK, :]
        return carry

    lax.fori_loop(0, chunks, chunk_body, 0)


def _attention(q, ka, kb, va, vb, sinks, *, tq, mask_history):
    b, s, q_dim = q.shape
    rows, kvw = ka.shape[1:]
    n_kv = kvw // LANES
    span = WINDOW + CHUNK
    hist_spec = pl.BlockSpec((None, rows, kvw), lambda i, c: (i, 0, 0))
    kern = functools.partial(_attn_kernel, n_kv=n_kv, chunks=tq // CHUNK, mask_history=mask_history)
    return pl.pallas_call(
        kern,
        grid=(b, s // tq),
        in_specs=[
            pl.BlockSpec(memory_space=pltpu.SMEM),
            pl.BlockSpec((None, tq, q_dim), lambda i, c: (i, c, 0)),
            hist_spec, hist_spec, hist_spec, hist_spec,
        ],
        out_specs=pl.BlockSpec((None, tq, q_dim), lambda i, c: (i, c, 0)),
        out_shape=jax.ShapeDtypeStruct((b, s, q_dim), BF16),
        scratch_shapes=[pltpu.VMEM((n_kv, 2, (GQA_GROUP // 2) * CHUNK, span), F32)],
        compiler_params=_params("parallel", "arbitrary"),
        name="swa_attention",
    )(sinks, q, ka, kb, va, vb)


def _pair_layouts(x3, n_kv):
    b, r, _ = x3.shape
    x4 = x3.astype(BF16).reshape(b, r, n_kv, HEAD_DIM)
    lo = jnp.pad(x4, ((0, 0), (0, 0), (0, 0), (0, LANES - HEAD_DIM)))
    hi = jnp.pad(x4, ((0, 0), (0, 0), (0, 0), (LANES - HEAD_DIM, 0)))
    return lo.reshape(b, r, n_kv * LANES), hi.reshape(b, r, n_kv * LANES)


def _oproj_kernel(o_ref, x_ref, w_ref, g_ref, y_ref):
    mix = jnp.dot(o_ref[...], w_ref[...], preferred_element_type=F32)
    y_ref[...] = x_ref[...] + _rms(mix, g_ref[...])


def _out_proj(o, x, w_bf16, g, *, tm):
    m, d = x.shape
    return pl.pallas_call(
        _oproj_kernel,
        grid=(m // tm,),
        in_specs=[
            pl.BlockSpec((tm, o.shape[1]), lambda i: (i, 0)),
            pl.BlockSpec((tm, d), lambda i: (i, 0)),
            _resident(w_bf16.shape),
            _resident((1, d)),
        ],
        out_specs=pl.BlockSpec((tm, d), lambda i: (i, 0)),
        out_shape=jax.ShapeDtypeStruct((m, d), F32),
        compiler_params=_params("parallel"),
        name="attn_out_proj",
    )(o, x, w_bf16, g)


def _pool_kernel(x_ref, prev_ref, gpre_ref, gpost_ref, w_ref, scale_ref, y_ref, tail_ref,
                 *, tm, pos0, prev_is_normed, group_dim):
    i = pl.program_id(1)
    gpre = gpre_ref[...]
    x = x_ref[...]
    h = _rms(x, gpre)
    tail_ref[...] = h[tm - HALO:, :]
    if prev_is_normed:
        prev = prev_ref[...]
    else:
        prev = jnp.where(i > 0, _rms(prev_ref[...], gpre), 0.0)
    pos = pos0 + i * tm + lax.broadcasted_iota(jnp.int32, (tm, 1), 0)
    outs = []
    for gi, w in enumerate(POOL_WINDOWS):
        sl = slice(gi * group_dim, (gi + 1) * group_dim)
        hg = h[:, sl]
        acc = jnp.concatenate([prev[:, sl], hg], axis=0)
        shift = 1
        while shift < w:
            acc = acc + pltpu.roll(acc, shift, 0)
            shift *= 2
        cnt = jnp.minimum(pos + 1, w).astype(F32)
        dgrp = (acc[HALO:, :] / cnt - hg).astype(BF16)
        outs.append(jnp.dot(dgrp, w_ref[gi], preferred_element_type=F32))
    mix = jnp.concatenate(outs, axis=1) * scale_ref[...]
    y_ref[...] = x + _rms(mix, gpost_ref[...])


def _pool_mixer(x, prev, gpre, gpost, w_bf16, scale, *, tm, pos0, prev_is_normed):
    b, s, d = x.shape
    nt = s // tm
    group_dim = d // len(POOL_WINDOWS)
    if prev_is_normed:
        prev_spec = pl.BlockSpec((None, HALO, d), lambda bi, i: (bi, 0, 0))
    else:
        per = tm // HALO
        prev_spec = pl.BlockSpec((None, HALO, d), lambda bi, i: (bi, jnp.maximum(i * per - 1, 0), 0))
    kern = functools.partial(_pool_kernel, tm=tm, pos0=pos0, prev_is_normed=prev_is_normed, group_dim=group_dim)
    return pl.pallas_call(
        kern,
        grid=(b, nt),
        in_specs=[
            pl.BlockSpec((None, tm, d), lambda bi, i: (bi, i, 0)),
            prev_spec,
            _resident((1, d)),
            _resident((1, d)),
            _resident(w_bf16.shape),
            _resident((1, d)),
        ],
        out_specs=[
            pl.BlockSpec((None, tm, d), lambda bi, i: (bi, i, 0)),
            pl.BlockSpec((None, None, HALO, d), lambda bi, i: (bi, i, 0, 0)),
        ],
        out_shape=[
            jax.ShapeDtypeStruct((b, s, d), F32),
            jax.ShapeDtypeStruct((b, nt, HALO, d), F32),
        ],
        compiler_params=_params("parallel", "parallel"),
        name="pool_mixer",
    )(x, prev, gpre, gpost, w_bf16, scale)


def _ffn_kernel(x_ref, p_ref, gpre_ref, gpost_ref, wup_ref, wdown_ref, wgate_ref, wproj_ref, y_ref,
                h_scr, acc_scr, *, nf):
    f = pl.program_id(1)

    @pl.when(f == 0)
    def _():
        h_scr[...] = _rms(x_ref[...], gpre_ref[...]).astype(BF16)
        acc_scr[...] = jnp.zeros_like(acc_scr)

    u = jnp.maximum(jnp.dot(h_scr[...], wup_ref[...], preferred_element_type=F32), 0.0)
    acc_scr[...] += jnp.dot((u * u).astype(BF16), wdown_ref[...], preferred_element_type=F32)

    @pl.when(f == nf - 1)
    def _():
        x1 = x_ref[...] + _rms(acc_scr[...], gpost_ref[...])
        gate = jax.nn.sigmoid(jnp.dot(x1.astype(BF16), wgate_ref[...], preferred_element_type=F32))
        emb = jnp.dot(p_ref[...].astype(BF16), wproj_ref[...], preferred_element_type=F32)
        y_ref[...] = x1 + gate * emb


def _ffn_ple(x, p, gpre, gpost, wup, wdown, wgate, wproj, *, layer, tm, tf):
    m, d = x.shape
    dff = wup.shape[2]
    nf = dff // tf
    return pl.pallas_call(
        functools.partial(_ffn_kernel, nf=nf),
        grid=(m // tm, nf),
        in_specs=[
            pl.BlockSpec((tm, d), lambda i, f: (i, 0)),
            pl.BlockSpec((None, tm, p.shape[2]), lambda i, f: (layer, i, 0)),
            _resident((1, d)),
            _resident((1, d)),
            pl.BlockSpec((None, d, tf), lambda i, f: (layer, 0, f)),
            pl.BlockSpec((None, tf, d), lambda i, f: (layer, f, 0)),
            _resident(wgate.shape, layer),
            _resident(wproj.shape, layer),
        ],
        out_specs=pl.BlockSpec((tm, d), lambda i, f: (i, 0)),
        out_shape=jax.ShapeDtypeStruct((m, d), F32),
        scratch_shapes=[pltpu.VMEM((tm, d), BF16), pltpu.VMEM((tm, d), F32)],
        compiler_params=_params("parallel", "arbitrary"),
        name="ffn_ple",
    )(x, p, gpre, gpost, wup, wdown, wgate, wproj)


def _row(v):
    return v.reshape(1, -1)


def kernel(x_prompt, x_sample, cache_k, cache_v, state_pool, p_prompt, p_sample, norm_mix_pre, norm_mix_post, norm_ffn_pre, norm_ffn_post, w_qkv, b_qkv, w_o, sinks, w_pool, pool_scale, w_ffn_up, w_ffn_down, w_ple_proj, w_ple_gate):
    b, s, d = x_prompt.shape
    bs, ts, _ = x_sample.shape
    q_dim = w_o.shape[1]
    kv_dim = (w_qkv.shape[2] - q_dim) // 2
    n_kv = kv_dim // HEAD_DIM
    tm = 512
    assert s % tm == 0 and (bs * ts) % tm == 0 and tm % ts == 0

    def bf(w):
        return w.astype(BF16)

    xp = x_prompt.reshape(b * s, d)
    xs = x_sample.reshape(bs * ts, d)
    wup, wdown, wgate, wproj = bf(w_ffn_up), bf(w_ffn_down), bf(w_ple_gate), bf(w_ple_proj)
    pp = p_prompt.reshape(p_prompt.shape[0], b * s, -1)
    ps = p_sample.reshape(p_sample.shape[0], bs * ts, -1)

    def ffn(x, p, i):
        return _ffn_ple(x, p, _row(norm_ffn_pre[i]), _row(norm_ffn_post[i]), wup, wdown, wgate, wproj,
                        layer=i, tm=tm, tf=512)

    tab_p = _rope_tables(jnp.arange(s, dtype=jnp.int32))
    tab_s = tuple(jnp.tile(t, (tm // ts, 1)) for t in _rope_tables(PAST_LEN + jnp.arange(ts, dtype=jnp.int32)))
    wqkv, bqkv, wo = bf(w_qkv[0]), _row(b_qkv[0]), bf(w_o[0])
    g_pre, g_post = _row(norm_mix_pre[0]), _row(norm_mix_post[0])

    q_p, k_p, v_p = _qkv_rope(xp, g_pre, wqkv, bqkv, tab_p, tm=tm, q_dim=q_dim, kv_dim=kv_dim)
    q_s, k_s, v_s = _qkv_rope(xs, g_pre, wqkv, bqkv, tab_s, tm=tm, q_dim=q_dim, kv_dim=kv_dim)

    k_p3, v_p3 = k_p.reshape(b, s, kv_dim), v_p.reshape(b, s, kv_dim)
    k_s3, v_s3 = k_s.reshape(bs, ts, kv_dim), v_s.reshape(bs, ts, kv_dim)
    pad = ((0, 0), (WINDOW, 0), (0, 0))
    o_p = _attention(q_p.reshape(b, s, q_dim), *_pair_layouts(jnp.pad(k_p3, pad), n_kv),
                     *_pair_layouts(jnp.pad(v_p3, pad), n_kv), sinks[0], tq=256, mask_history=True)
    ck = jnp.concatenate([cache_k[0].reshape(bs, WINDOW, kv_dim), k_s3], axis=1)
    cv = jnp.concatenate([cache_v[0].reshape(bs, WINDOW, kv_dim), v_s3], axis=1)
    o_s = _attention(q_s.reshape(bs, ts, q_dim), *_pair_layouts(ck, n_kv), *_pair_layouts(cv, n_kv), sinks[0],
                     tq=ts, mask_history=False)

    xp = _out_proj(o_p.reshape(b * s, q_dim), xp, wo, g_post, tm=tm)
    xs = _out_proj(o_s.reshape(bs * ts, q_dim), xs, wo, g_post, tm=tm)

    xp = ffn(xp, pp, 0)
    xs = ffn(xs, ps, 0)

    g_pre, g_post = _row(norm_mix_pre[1]), _row(norm_mix_post[1])
    wpool, pscale = bf(w_pool[0]), _row(pool_scale[0])
    xp3, tail_p = _pool_mixer(xp.reshape(b, s, d), xp.reshape(b, s, d), g_pre, g_post, wpool, pscale,
                              tm=256, pos0=0, prev_is_normed=False)
    hist = jnp.pad(state_pool[0], ((0, 0), (HALO - POOL_HIST, 0), (0, 0)))
    xs3, tail_s = _pool_mixer(xs.reshape(bs, ts, d), hist, g_pre, g_post, wpool, pscale,
                              tm=ts, pos0=PAST_LEN, prev_is_normed=True)

    xp = ffn(xp3.reshape(b * s, d), pp, 1)
    xs = ffn(xs3.reshape(bs * ts, d), ps, 1)

    n_heads_kv = (n_kv, HEAD_DIM)
    return (
        xp.reshape(b, s, d),
        xs.reshape(bs, ts, d),
        k_p3[:, s - WINDOW:].reshape(1, b, WINDOW, *n_heads_kv),
        v_p3[:, s - WINDOW:].reshape(1, b, WINDOW, *n_heads_kv),
        tail_p[:, -1, HALO - POOL_HIST:][None],
        k_s3.reshape(1, bs, ts, *n_heads_kv),
        v_s3.reshape(1, bs, ts, *n_heads_kv),
        tail_s[:, -1, HALO - POOL_HIST:][None],
    )
```

```python
import functools
import math

import jax
import jax.numpy as jnp
from jax import lax
from jax.experimental import pallas as pl
from jax.experimental.pallas import tpu as pltpu

HEAD_DIM = 64
GQA_GROUP = 8
CHUNK = 64
WINDOW = 128
ROT_DIM = HEAD_DIM // 4
ROPE_THETA = 500000.0
POOL_WINDOWS = (2, 4, 8, 16)
POOL_HIST = max(POOL_WINDOWS) - 1
PAST_LEN = 2048
EPS = 1e-6

LANES = 128
HALO = 16
VMEM_LIMIT_BYTES = 60 * 1024 * 1024

F32 = jnp.float32
BF16 = jnp.bfloat16


def _rms(x, g):
    ms = jnp.mean(x * x, axis=-1, keepdims=True)
    return x * lax.rsqrt(ms + EPS) * g


def _resident(shape, layer=None):
    if layer is None:
        nd = len(shape)
        return pl.BlockSpec(shape, lambda *_: (0,) * nd, pipeline_mode=pl.Buffered(1))
    nd = len(shape) - 1
    return pl.BlockSpec((None,) + tuple(shape[1:]), lambda *_: (layer,) + (0,) * nd, pipeline_mode=pl.Buffered(1))


def _params(*sem):
    return pltpu.CompilerParams(dimension_semantics=sem, vmem_limit_bytes=VMEM_LIMIT_BYTES)


def _rope_tables(positions):
    half = ROT_DIM // 2
    inv = ROPE_THETA ** (-jnp.arange(0, ROT_DIM, 2, dtype=F32) / ROT_DIM)
    ang = positions.astype(F32)[:, None] * inv[None, :]
    cos, sin = jnp.cos(ang), jnp.sin(ang)
    p = positions.shape[0]
    ones = jnp.ones((p, HEAD_DIM - ROT_DIM), F32)
    zeros = jnp.zeros((p, HEAD_DIM - ROT_DIM), F32)
    zh = jnp.zeros((p, half), F32)
    c = jnp.concatenate([cos, cos, ones], axis=1)
    sa = jnp.concatenate([-sin, zh, zeros], axis=1)
    sb = jnp.concatenate([zh, sin, zeros], axis=1)
    rep = LANES // HEAD_DIM
    return tuple(jnp.tile(t, (1, rep)) for t in (c, sa, sb))


def _store_pair_layouts(blk, a_ref, b_ref, col):
    lo = lax.broadcasted_iota(jnp.int32, blk.shape, 1) < HEAD_DIM
    swapped = pltpu.roll(blk, HEAD_DIM, 1)
    zero = jnp.zeros_like(blk)
    a_ref[:, col:col + LANES] = jnp.where(lo, blk, zero).astype(BF16)
    b_ref[:, col:col + LANES] = jnp.where(lo, zero, swapped).astype(BF16)
    a_ref[:, col + LANES:col + 2 * LANES] = jnp.where(lo, swapped, zero).astype(BF16)
    b_ref[:, col + LANES:col + 2 * LANES] = jnp.where(lo, zero, blk).astype(BF16)


def _qkv_kernel(x_ref, g_ref, w_ref, b_ref, c_ref, sa_ref, sb_ref,
                q_ref, k_ref, v_ref, ka_ref, kb_ref, va_ref, vb_ref, *, q_dim, kv_dim, nchunk):
    h = _rms(x_ref[...], g_ref[...]).astype(BF16)
    c, sa, sb = c_ref[...], sa_ref[...], sb_ref[...]
    scale = 1.0 / math.sqrt(HEAD_DIM)
    cq, saq, sbq = c * scale, sa * scale, sb * scale
    half = ROT_DIM // 2
    heads_per_blk = LANES // HEAD_DIM
    rot_cols = q_dim + kv_dim
    total = q_dim + 2 * kv_dim
    for n0 in range(0, total, nchunk):
        acc = jnp.dot(h, w_ref[:, n0:n0 + nchunk], preferred_element_type=F32) + b_ref[:, n0:n0 + nchunk]
        for j0 in range(0, nchunk, LANES):
            col = n0 + j0
            blk = acc[:, j0:j0 + LANES]
            if col < q_dim:
                blk = blk * cq + pltpu.roll(blk, LANES - half, 1) * saq + pltpu.roll(blk, half, 1) * sbq
                q_ref[:, col:col + LANES] = blk.astype(BF16)
            elif col < rot_cols:
                blk = blk * c + pltpu.roll(blk, LANES - half, 1) * sa + pltpu.roll(blk, half, 1) * sb
                k_ref[:, col - q_dim:col - q_dim + LANES] = blk
                _store_pair_layouts(blk, ka_ref, kb_ref, (col - q_dim) * heads_per_blk)
            else:
                v_ref[:, col - rot_cols:col - rot_cols + LANES] = blk
                _store_pair_layouts(blk, va_ref, vb_ref, (col - rot_cols) * heads_per_blk)


def _qkv_rope(x, g, w_bf16, b, tables, *, tm, q_dim, kv_dim):
    m, d = x.shape
    total = q_dim + 2 * kv_dim
    ptiles = tables[0].shape[0] // tm
    tab_spec = pl.BlockSpec((tm, LANES), lambda i: (i % ptiles, 0))
    kern = functools.partial(_qkv_kernel, q_dim=q_dim, kv_dim=kv_dim, nchunk=256)
    pair_w = kv_dim * (LANES // HEAD_DIM)

    def rows(width):
        return pl.BlockSpec((tm, width), lambda i: (i, 0))

    return pl.pallas_call(
        kern,
        grid=(m // tm,),
        in_specs=[
            rows(d),
            _resident((1, d)),
            _resident((d, total)),
            _resident((1, total)),
            tab_spec, tab_spec, tab_spec,
        ],
        out_specs=[rows(q_dim), rows(kv_dim), rows(kv_dim), rows(pair_w), rows(pair_w), rows(pair_w), rows(pair_w)],
        out_shape=[
            jax.ShapeDtypeStruct((m, q_dim), BF16),
            jax.ShapeDtypeStruct((m, kv_dim), F32),
            jax.ShapeDtypeStruct((m, kv_dim), F32),
        ] + [jax.ShapeDtypeStruct((m, pair_w), BF16)] * 4,
        compiler_params=_params("parallel"),
        name="qkv_rope",
    )(x, g, w_bf16, b, *tables)


def _attn_kernel(sink_ref, q_ref, ka_ref, kb_ref, va_ref, vb_ref, o_ref, s_scr, *, n_kv, chunks, hist_chunks):
    step = pl.program_id(1)
    span = WINDOW + CHUNK
    pairs = GQA_GROUP // 2
    lane = lax.broadcasted_iota(jnp.int32, (span, LANES), 1)
    ones_a = jnp.where(lane < HEAD_DIM, 1.0, 0.0).astype(BF16)
    ones_b = jnp.where(lane >= HEAD_DIM, 1.0, 0.0).astype(BF16)
    is_a = lax.broadcasted_iota(jnp.int32, (pairs * CHUNK, LANES), 1) < HEAD_DIM
    nt = (((1,), (1,)), ((), ()))

    def chunk_body(ci, carry):
        row0 = pl.multiple_of(ci * CHUNK, CHUNK)
        ckv = step * chunks + ci + hist_chunks
        k0 = pl.multiple_of(jnp.maximum(ckv - WINDOW // CHUNK, 0) * CHUNK, CHUNK)
        rows = pl.ds(row0, CHUNK)
        win = pl.ds(k0, span)

        for kh in range(n_kv):
            kcols = slice(kh * LANES, (kh + 1) * LANES)
            q_pairs = jnp.concatenate(
                [q_ref[rows, (kh * pairs + p) * LANES:(kh * pairs + p + 1) * LANES] for p in range(pairs)], axis=0)
            s_scr[kh, 0] = lax.dot_general(q_pairs, ka_ref[win, kcols], nt, preferred_element_type=F32)
            s_scr[kh, 1] = lax.dot_general(q_pairs, kb_ref[win, kcols], nt, preferred_element_type=F32)

        if hist_chunks < WINDOW // CHUNK:
            @pl.when(ckv < WINDOW // CHUNK)
            def _():
                key_row = lax.broadcasted_iota(jnp.int32, (1, span), 1)
                bias = jnp.where(key_row < (ckv + 1) * CHUNK, 0.0, -jnp.inf)
                for kh in range(n_kv):
                    for ab in range(2):
                        s_scr[kh, ab] = s_scr[kh, ab] + bias

        e_all, t_all = [], []
        for kh in range(n_kv):
            for ab in range(2):
                e_rows, t_rows = [], []
                for p in range(pairs):
                    sk = sink_ref[kh * GQA_GROUP + 2 * p + ab]
                    s = s_scr[kh, ab, p * CHUNK:(p + 1) * CHUNK, :]
                    m = jnp.maximum(jnp.max(s, axis=-1, keepdims=True), sk)
                    e_rows.append(jnp.exp(s - m).astype(BF16))
                    t_rows.append(jnp.exp(sk - m))
                e_all.append(jnp.concatenate(e_rows, axis=0))
                t_all.append(jnp.concatenate(t_rows, axis=0))

        for kh in range(n_kv):
            kcols = slice(kh * LANES, (kh + 1) * LANES)
            rhs_a = jnp.concatenate([va_ref[win, kcols], ones_a], axis=1)
            rhs_b = jnp.concatenate([vb_ref[win, kcols], ones_b], axis=1)
            acc = (jnp.dot(e_all[2 * kh], rhs_a, preferred_element_type=F32)
                   + jnp.dot(e_all[2 * kh + 1], rhs_b, preferred_element_type=F32))
            den = acc[:, LANES:] + jnp.where(is_a, t_all[2 * kh], t_all[2 * kh + 1])
            o_pairs = (acc[:, :LANES] / den).astype(BF16)
            for p in range(pairs):
                col = (kh * pairs + p) * LANES
                o_ref[rows, col:col + LANES] = o_pairs[p * CHUNK:(p + 1) * CHUNK, :]
        return carry

    lax.fori_loop(0, chunks, chunk_body, 0)


def _attention(q, ka, kb, va, vb, sinks, *, tq):
    b, s, q_dim = q.shape
    rows, kvw = ka.shape[1:]
    n_kv = kvw // LANES
    span = WINDOW + CHUNK
    assert rows >= span and (rows - s) % CHUNK == 0
    hist_spec = pl.BlockSpec((None, rows, kvw), lambda i, c: (i, 0, 0))
    kern = functools.partial(_attn_kernel, n_kv=n_kv, chunks=tq // CHUNK, hist_chunks=(rows - s) // CHUNK)
    return pl.pallas_call(
        kern,
        grid=(b, s // tq),
        in_specs=[
            pl.BlockSpec(memory_space=pltpu.SMEM),
            pl.BlockSpec((None, tq, q_dim), lambda i, c: (i, c, 0)),
            hist_spec, hist_spec, hist_spec, hist_spec,
        ],
        out_specs=pl.BlockSpec((None, tq, q_dim), lambda i, c: (i, c, 0)),
        out_shape=jax.ShapeDtypeStruct((b, s, q_dim), BF16),
        scratch_shapes=[pltpu.VMEM((n_kv, 2, (GQA_GROUP // 2) * CHUNK, span), F32)],
        compiler_params=_params("parallel", "arbitrary"),
        name="swa_attention",
    )(sinks, q, ka, kb, va, vb)


def _pair_layouts(x3, n_kv):
    b, r, _ = x3.shape
    x4 = x3.astype(BF16).reshape(b, r, n_kv, HEAD_DIM)
    lo = jnp.pad(x4, ((0, 0), (0, 0), (0, 0), (0, LANES - HEAD_DIM)))
    hi = jnp.pad(x4, ((0, 0), (0, 0), (0, 0), (LANES - HEAD_DIM, 0)))
    return lo.reshape(b, r, n_kv * LANES), hi.reshape(b, r, n_kv * LANES)


def _oproj_kernel(o_ref, x_ref, w_ref, g_ref, y_ref):
    mix = jnp.dot(o_ref[...], w_ref[...], preferred_element_type=F32)
    y_ref[...] = x_ref[...] + _rms(mix, g_ref[...])


def _out_proj(o, x, w_bf16, g, *, tm):
    m, d = x.shape
    return pl.pallas_call(
        _oproj_kernel,
        grid=(m // tm,),
        in_specs=[
            pl.BlockSpec((tm, o.shape[1]), lambda i: (i, 0)),
            pl.BlockSpec((tm, d), lambda i: (i, 0)),
            _resident(w_bf16.shape),
            _resident((1, d)),
        ],
        out_specs=pl.BlockSpec((tm, d), lambda i: (i, 0)),
        out_shape=jax.ShapeDtypeStruct((m, d), F32),
        compiler_params=_params("parallel"),
        name="attn_out_proj",
    )(o, x, w_bf16, g)


def _pool_kernel(x_ref, prev_ref, gpre_ref, gpost_ref, w_ref, scale_ref, y_ref, tail_ref,
                 *, tm, pos0, prev_is_normed, group_dim):
    i = pl.program_id(1)
    gpre = gpre_ref[...]
    x = x_ref[...]
    h = _rms(x, gpre)
    tail_ref[...] = h[tm - HALO:, :]
    if prev_is_normed:
        prev = prev_ref[...]
    else:
        prev = jnp.where(i > 0, _rms(prev_ref[...], gpre), 0.0)
    pos = pos0 + i * tm + lax.broadcasted_iota(jnp.int32, (tm, 1), 0)
    outs = []
    for gi, w in enumerate(POOL_WINDOWS):
        sl = slice(gi * group_dim, (gi + 1) * group_dim)
        hg = h[:, sl]
        acc = jnp.concatenate([prev[:, sl], hg], axis=0)
        shift = 1
        while shift < w:
            acc = acc + pltpu.roll(acc, shift, 0)
            shift *= 2
        cnt = jnp.minimum(pos + 1, w).astype(F32)
        dgrp = (acc[HALO:, :] / cnt - hg).astype(BF16)
        outs.append(jnp.dot(dgrp, w_ref[gi], preferred_element_type=F32))
    mix = jnp.concatenate(outs, axis=1) * scale_ref[...]
    y_ref[...] = x + _rms(mix, gpost_ref[...])


def _pool_mixer(x, prev, gpre, gpost, w_bf16, scale, *, tm, pos0, prev_is_normed):
    b, s, d = x.shape
    nt = s // tm
    group_dim = d // len(POOL_WINDOWS)
    if prev_is_normed:
        prev_spec = pl.BlockSpec((None, HALO, d), lambda bi, i: (bi, 0, 0))
    else:
        per = tm // HALO
        prev_spec = pl.BlockSpec((None, HALO, d), lambda bi, i: (bi, jnp.maximum(i * per - 1, 0), 0))
    kern = functools.partial(_pool_kernel, tm=tm, pos0=pos0, prev_is_normed=prev_is_normed, group_dim=group_dim)
    return pl.pallas_call(
        kern,
        grid=(b, nt),
        in_specs=[
            pl.BlockSpec((None, tm, d), lambda bi, i: (bi, i, 0)),
            prev_spec,
            _resident((1, d)),
            _resident((1, d)),
            _resident(w_bf16.shape),
            _resident((1, d)),
        ],
        out_specs=[
            pl.BlockSpec((None, tm, d), lambda bi, i: (bi, i, 0)),
            pl.BlockSpec((None, None, HALO, d), lambda bi, i: (bi, i, 0, 0)),
        ],
        out_shape=[
            jax.ShapeDtypeStruct((b, s, d), F32),
            jax.ShapeDtypeStruct((b, nt, HALO, d), F32),
        ],
        compiler_params=_params("parallel", "parallel"),
        name="pool_mixer",
    )(x, prev, gpre, gpost, w_bf16, scale)


def _ffn_kernel(x_ref, p_ref, gpre_ref, gpost_ref, wup_ref, wdown_ref, wgate_ref, wproj_ref, y_ref, h_scr, *, nf):
    f = pl.program_id(1)

    @pl.when(f == 0)
    def _():
        h_scr[...] = _rms(x_ref[...], gpre_ref[...]).astype(BF16)
        y_ref[...] = jnp.zeros_like(y_ref)

    u = jnp.maximum(jnp.dot(h_scr[...], wup_ref[...], preferred_element_type=F32), 0.0)
    y_ref[...] += jnp.dot((u * u).astype(BF16), wdown_ref[...], preferred_element_type=F32)

    @pl.when(f == nf - 1)
    def _():
        x1 = x_ref[...] + _rms(y_ref[...], gpost_ref[...])
        gate = jax.nn.sigmoid(jnp.dot(x1.astype(BF16), wgate_ref[...], preferred_element_type=F32))
        emb = jnp.dot(p_ref[...].astype(BF16), wproj_ref[...], preferred_element_type=F32)
        y_ref[...] = x1 + gate * emb


def _ffn_ple(x, p, gpre, gpost, wup, wdown, wgate, wproj, *, layer, tm, tf):
    m, d = x.shape
    dff = wup.shape[2]
    nf = dff // tf
    return pl.pallas_call(
        functools.partial(_ffn_kernel, nf=nf),
        grid=(m // tm, nf),
        in_specs=[
            pl.BlockSpec((tm, d), lambda i, f: (i, 0)),
            pl.BlockSpec((None, tm, p.shape[2]), lambda i, f: (layer, i, 0)),
            _resident((1, d)),
            _resident((1, d)),
            pl.BlockSpec((None, d, tf), lambda i, f: (layer, 0, f)),
            pl.BlockSpec((None, tf, d), lambda i, f: (layer, f, 0)),
            _resident(wgate.shape, layer),
            _resident(wproj.shape, layer),
        ],
        out_specs=pl.BlockSpec((tm, d), lambda i, f: (i, 0)),
        out_shape=jax.ShapeDtypeStruct((m, d), F32),
        scratch_shapes=[pltpu.VMEM((tm, d), BF16)],
        compiler_params=_params("parallel", "arbitrary"),
        name="ffn_ple",
    )(x, p, gpre, gpost, wup, wdown, wgate, wproj)


def _row(v):
    return v.reshape(1, -1)


def kernel(x_prompt, x_sample, cache_k, cache_v, state_pool, p_prompt, p_sample, norm_mix_pre, norm_mix_post, norm_ffn_pre, norm_ffn_post, w_qkv, b_qkv, w_o, sinks, w_pool, pool_scale, w_ffn_up, w_ffn_down, w_ple_proj, w_ple_gate):
    b, s, d = x_prompt.shape
    bs, ts, _ = x_sample.shape
    q_dim = w_o.shape[1]
    kv_dim = (w_qkv.shape[2] - q_dim) // 2
    n_kv = kv_dim // HEAD_DIM
    tm = 512
    assert s % tm == 0 and (bs * ts) % tm == 0 and tm % ts == 0

    def bf(w):
        return w.astype(BF16)

    xp = x_prompt.reshape(b * s, d)
    xs = x_sample.reshape(bs * ts, d)
    wup, wdown, wgate, wproj = bf(w_ffn_up), bf(w_ffn_down), bf(w_ple_gate), bf(w_ple_proj)
    pp = p_prompt.reshape(p_prompt.shape[0], b * s, -1)
    ps = p_sample.reshape(p_sample.shape[0], bs * ts, -1)

    def ffn(x, p, i):
        return _ffn_ple(x, p, _row(norm_ffn_pre[i]), _row(norm_ffn_post[i]), wup, wdown, wgate, wproj,
                        layer=i, tm=tm, tf=1024)

    tab_p = _rope_tables(jnp.arange(s, dtype=jnp.int32))
    tab_s = tuple(jnp.tile(t, (tm // ts, 1)) for t in _rope_tables(PAST_LEN + jnp.arange(ts, dtype=jnp.int32)))
    wqkv, bqkv, wo = bf(w_qkv[0]), _row(b_qkv[0]), bf(w_o[0])
    g_pre, g_post = _row(norm_mix_pre[0]), _row(norm_mix_post[0])

    q_p, k_p, v_p, *kv_p = _qkv_rope(xp, g_pre, wqkv, bqkv, tab_p, tm=tm, q_dim=q_dim, kv_dim=kv_dim)
    q_s, k_s, v_s, *kv_s = _qkv_rope(xs, g_pre, wqkv, bqkv, tab_s, tm=tm, q_dim=q_dim, kv_dim=kv_dim)

    k_p3, v_p3 = k_p.reshape(b, s, kv_dim), v_p.reshape(b, s, kv_dim)
    k_s3, v_s3 = k_s.reshape(bs, ts, kv_dim), v_s.reshape(bs, ts, kv_dim)
    o_p = _attention(q_p.reshape(b, s, q_dim), *(t.reshape(b, s, -1) for t in kv_p), sinks[0], tq=256)
    cache = (*_pair_layouts(cache_k[0].reshape(bs, WINDOW, kv_dim), n_kv),
             *_pair_layouts(cache_v[0].reshape(bs, WINDOW, kv_dim), n_kv))
    kv_s = (jnp.concatenate([c, t.reshape(bs, ts, -1)], axis=1) for c, t in zip(cache, kv_s))
    o_s = _attention(q_s.reshape(bs, ts, q_dim), *kv_s, sinks[0], tq=ts)

    xp = _out_proj(o_p.reshape(b * s, q_dim), xp, wo, g_post, tm=tm)
    xs = _out_proj(o_s.reshape(bs * ts, q_dim), xs, wo, g_post, tm=tm)

    xp = ffn(xp, pp, 0)
    xs = ffn(xs, ps, 0)

    g_pre, g_post = _row(norm_mix_pre[1]), _row(norm_mix_post[1])
    wpool, pscale = bf(w_pool[0]), _row(pool_scale[0])
    xp3, tail_p = _pool_mixer(xp.reshape(b, s, d), xp.reshape(b, s, d), g_pre, g_post, wpool, pscale,
                              tm=256, pos0=0, prev_is_normed=False)
    hist = jnp.pad(state_pool[0], ((0, 0), (HALO - POOL_HIST, 0), (0, 0)))
    xs3, tail_s = _pool_mixer(xs.reshape(bs, ts, d), hist, g_pre, g_post, wpool, pscale,
                              tm=ts, pos0=PAST_LEN, prev_is_normed=True)

    xp = ffn(xp3.reshape(b * s, d), pp, 1)
    xs = ffn(xs3.reshape(bs * ts, d), ps, 1)

    n_heads_kv = (n_kv, HEAD_DIM)
    return (
        xp.reshape(b, s, d),
        xs.reshape(bs, ts, d),
        k_p3[:, s - WINDOW:].reshape(1, b, WINDOW, *n_heads_kv),
        v_p3[:, s - WINDOW:].reshape(1, b, WINDOW, *n_heads_kv),
        tail_p[:, -1, HALO - POOL_HIST:][None],
        k_s3.reshape(1, bs, ts, *n_heads_kv),
        v_s3.reshape(1, bs, ts, *n_heads_kv),
        tail_s[:, -1, HALO - POOL_HIST:][None],
    )
```

```python
import functools
import math

import jax
import jax.numpy as jnp
from jax import lax
from jax.experimental import pallas as pl
from jax.experimental.pallas import tpu as pltpu

HEAD_DIM = 64
GQA_GROUP = 8
CHUNK = 64
WINDOW = 128
ROT_DIM = HEAD_DIM // 4
ROPE_THETA = 500000.0
POOL_WINDOWS = (2, 4, 8, 16)
POOL_HIST = max(POOL_WINDOWS) - 1
PAST_LEN = 2048
EPS = 1e-6

LANES = 128
BF16_ROWS = 16
HALO = 16
VMEM_LIMIT_BYTES = 60 * 1024 * 1024

F32 = jnp.float32
BF16 = jnp.bfloat16


def _rms(x, g):
    ms = jnp.mean(x * x, axis=-1, keepdims=True)
    return x * lax.rsqrt(ms + EPS) * g


def _resident(shape, layer=None):
    if layer is None:
        nd = len(shape)
        return pl.BlockSpec(shape, lambda *_: (0,) * nd, pipeline_mode=pl.Buffered(1))
    nd = len(shape) - 1
    return pl.BlockSpec((None,) + tuple(shape[1:]), lambda *_: (layer,) + (0,) * nd, pipeline_mode=pl.Buffered(1))


def _params(*sem):
    return pltpu.CompilerParams(dimension_semantics=sem, vmem_limit_bytes=VMEM_LIMIT_BYTES)


def _cast_specs(casts, n_steps, flat_step):
    in_specs, out_specs, out_shapes, operands = [], [], [], []
    for w, layer in casts:
        _, r, c = w.shape
        rb = max(BF16_ROWS, r // n_steps)
        csplit = rb * n_steps // r
        assert r % rb == 0 and (r // rb) * csplit == n_steps and c % (csplit * LANES) == 0
        cw = c // csplit

        def in_map(*idx, layer=layer, csplit=csplit):
            k = flat_step(*idx)
            return (layer, k // csplit, k % csplit)

        def out_map(*idx, csplit=csplit):
            k = flat_step(*idx)
            return (k // csplit, k % csplit)

        in_specs.append(pl.BlockSpec((None, rb, cw), in_map))
        out_specs.append(pl.BlockSpec((rb, cw), out_map))
        out_shapes.append(jax.ShapeDtypeStruct((r, c), BF16))
        operands.append(w)
    return in_specs, out_specs, out_shapes, operands


def _run_casts(srcs, dsts):
    for src, dst in zip(srcs, dsts, strict=True):
        dst[...] = src[...].astype(BF16)


def _rope_tables(positions):
    half = ROT_DIM // 2
    inv = ROPE_THETA ** (-jnp.arange(0, ROT_DIM, 2, dtype=F32) / ROT_DIM)
    ang = positions.astype(F32)[:, None] * inv[None, :]
    cos, sin = jnp.cos(ang), jnp.sin(ang)
    p = positions.shape[0]
    ones = jnp.ones((p, HEAD_DIM - ROT_DIM), F32)
    zeros = jnp.zeros((p, HEAD_DIM - ROT_DIM), F32)
    zh = jnp.zeros((p, half), F32)
    c = jnp.concatenate([cos, cos, ones], axis=1)
    sa = jnp.concatenate([-sin, zh, zeros], axis=1)
    sb = jnp.concatenate([zh, sin, zeros], axis=1)
    rep = LANES // HEAD_DIM
    return tuple(jnp.tile(t, (1, rep)) for t in (c, sa, sb))


def _store_pair_layouts(blk, a_ref, b_ref, col):
    lo = lax.broadcasted_iota(jnp.int32, blk.shape, 1) < HEAD_DIM
    swapped = pltpu.roll(blk, HEAD_DIM, 1)
    zero = jnp.zeros_like(blk)
    a_ref[:, col:col + LANES] = jnp.where(lo, blk, zero).astype(BF16)
    b_ref[:, col:col + LANES] = jnp.where(lo, zero, swapped).astype(BF16)
    a_ref[:, col + LANES:col + 2 * LANES] = jnp.where(lo, swapped, zero).astype(BF16)
    b_ref[:, col + LANES:col + 2 * LANES] = jnp.where(lo, zero, blk).astype(BF16)


N_QKV_IN = 7
N_QKV_OUT = 7


def _qkv_kernel(*refs, q_dim, kv_dim, nchunk):
    x_ref, g_ref, w_ref, b_ref, c_ref, sa_ref, sb_ref = refs[:N_QKV_IN]
    n_cast = (len(refs) - N_QKV_IN - N_QKV_OUT) // 2
    cast_in = refs[N_QKV_IN:N_QKV_IN + n_cast]
    q_ref, k_ref, v_ref, ka_ref, kb_ref, va_ref, vb_ref = refs[N_QKV_IN + n_cast:N_QKV_IN + n_cast + N_QKV_OUT]
    cast_out = refs[N_QKV_IN + n_cast + N_QKV_OUT:]
    _run_casts(cast_in, cast_out)

    h = _rms(x_ref[...], g_ref[...]).astype(BF16)
    c, sa, sb = c_ref[...], sa_ref[...], sb_ref[...]
    scale = 1.0 / math.sqrt(HEAD_DIM)
    cq, saq, sbq = c * scale, sa * scale, sb * scale
    half = ROT_DIM // 2
    heads_per_blk = LANES // HEAD_DIM
    rot_cols = q_dim + kv_dim
    total = q_dim + 2 * kv_dim
    for n0 in range(0, total, nchunk):
        acc = jnp.dot(h, w_ref[:, n0:n0 + nchunk], preferred_element_type=F32) + b_ref[:, n0:n0 + nchunk]
        for j0 in range(0, nchunk, LANES):
            col = n0 + j0
            blk = acc[:, j0:j0 + LANES]
            if col < q_dim:
                blk = blk * cq + pltpu.roll(blk, LANES - half, 1) * saq + pltpu.roll(blk, half, 1) * sbq
                q_ref[:, col:col + LANES] = blk.astype(BF16)
            elif col < rot_cols:
                blk = blk * c + pltpu.roll(blk, LANES - half, 1) * sa + pltpu.roll(blk, half, 1) * sb
                k_ref[:, col - q_dim:col - q_dim + LANES] = blk
                _store_pair_layouts(blk, ka_ref, kb_ref, (col - q_dim) * heads_per_blk)
            else:
                v_ref[:, col - rot_cols:col - rot_cols + LANES] = blk
                _store_pair_layouts(blk, va_ref, vb_ref, (col - rot_cols) * heads_per_blk)


def _qkv_rope(x, g, w_bf16, b, tables, *, tm, q_dim, kv_dim, casts=()):
    m, d = x.shape
    total = q_dim + 2 * kv_dim
    ptiles = tables[0].shape[0] // tm
    tab_spec = pl.BlockSpec((tm, LANES), lambda i: (i % ptiles, 0))
    kern = functools.partial(_qkv_kernel, q_dim=q_dim, kv_dim=kv_dim, nchunk=256)
    pair_w = kv_dim * (LANES // HEAD_DIM)
    c_in, c_out, c_shapes, c_ops = _cast_specs(casts, m // tm, lambda i: i)

    def rows(width):
        return pl.BlockSpec((tm, width), lambda i: (i, 0))

    return pl.pallas_call(
        kern,
        grid=(m // tm,),
        in_specs=[
            rows(d),
            _resident((1, d)),
            _resident((d, total)),
            _resident((1, total)),
            tab_spec, tab_spec, tab_spec,
        ] + c_in,
        out_specs=[rows(q_dim), rows(kv_dim), rows(kv_dim), rows(pair_w), rows(pair_w), rows(pair_w), rows(pair_w)]
        + c_out,
        out_shape=[
            jax.ShapeDtypeStruct((m, q_dim), BF16),
            jax.ShapeDtypeStruct((m, kv_dim), F32),
            jax.ShapeDtypeStruct((m, kv_dim), F32),
        ] + [jax.ShapeDtypeStruct((m, pair_w), BF16)] * 4 + c_shapes,
        compiler_params=_params("parallel"),
        name="qkv_rope",
    )(x, g, w_bf16, b, *tables, *c_ops)


def _attn_kernel(sink_ref, q_ref, ka_ref, kb_ref, va_ref, vb_ref, o_ref, s_scr, *, n_kv, chunks, hist_chunks):
    step = pl.program_id(1)
    span = WINDOW + CHUNK
    pairs = GQA_GROUP // 2
    win_chunks = WINDOW // CHUNK
    group = 2 if chunks % 2 == 0 else 1
    neg_inf = -jnp.inf
    row_k = lax.broadcasted_iota(jnp.int32, (2 * span, LANES), 0)
    lane_k = lax.broadcasted_iota(jnp.int32, (2 * span, LANES), 1)
    ones_ab = jnp.where((row_k < span) == (lane_k < HEAD_DIM), 1.0, 0.0).astype(BF16)
    lo = lax.broadcasted_iota(jnp.int32, (CHUNK, LANES), 1) < HEAD_DIM
    nt = (((1,), (1,)), ((), ()))

    def body(j, carry, *, masked):
        units = []
        for u in range(group):
            ci = j * group + u
            ckv = step * chunks + ci + hist_chunks
            k0 = pl.multiple_of(jnp.maximum(ckv - win_chunks, 0) * CHUNK, CHUNK)
            units.append((pl.ds(pl.multiple_of(ci * CHUNK, CHUNK), CHUNK), pl.ds(k0, span), ckv))

        for u, (rows, win, _) in enumerate(units):
            for kh in range(n_kv):
                kcols = slice(kh * LANES, (kh + 1) * LANES)
                q_pairs = jnp.concatenate(
                    [q_ref[rows, (kh * pairs + p) * LANES:(kh * pairs + p + 1) * LANES] for p in range(pairs)], axis=0)
                k_ab = jnp.concatenate([ka_ref[win, kcols], kb_ref[win, kcols]], axis=0)
                s_scr[u, kh] = lax.dot_general(q_pairs, k_ab, nt, preferred_element_type=F32)

        e_all, t_all = [], []
        for u, (_, _, ckv) in enumerate(units):
            if masked:
                key_row = lax.broadcasted_iota(jnp.int32, (1, span), 1)
                bias1 = jnp.where(key_row < (ckv + 1) * CHUNK, 0.0, neg_inf)
                bias = jnp.concatenate([bias1, bias1], axis=1)
            for kh in range(n_kv):
                e_rows, t_rows = [], []
                for p in range(pairs):
                    s = s_scr[u, kh, p * CHUNK:(p + 1) * CHUNK, :]
                    if masked:
                        s = s + bias
                    t0, t1, t2 = s[:, :LANES], s[:, LANES:2 * LANES], s[:, 2 * LANES:]
                    sk_a = sink_ref[kh * GQA_GROUP + 2 * p]
                    sk_b = sink_ref[kh * GQA_GROUP + 2 * p + 1]
                    m_a = jnp.max(jnp.maximum(t0, jnp.where(lo, t1, neg_inf)), axis=-1, keepdims=True)
                    m_b = jnp.max(jnp.maximum(t2, jnp.where(lo, neg_inf, t1)), axis=-1, keepdims=True)
                    m_a, m_b = jnp.maximum(m_a, sk_a), jnp.maximum(m_b, sk_b)
                    e = jnp.concatenate(
                        [jnp.exp(t0 - m_a), jnp.exp(t1 - jnp.where(lo, m_a, m_b)), jnp.exp(t2 - m_b)], axis=1)
                    e_rows.append(e.astype(BF16))
                    t_rows.append(jnp.where(lo, jnp.exp(sk_a - m_a), jnp.exp(sk_b - m_b)))
                e_all.append(jnp.concatenate(e_rows, axis=0))
                t_all.append(jnp.concatenate(t_rows, axis=0))

        for u, (rows, win, _) in enumerate(units):
            for kh in range(n_kv):
                kcols = slice(kh * LANES, (kh + 1) * LANES)
                v_ab = jnp.concatenate([va_ref[win, kcols], vb_ref[win, kcols]], axis=0)
                rhs = jnp.concatenate([v_ab, ones_ab], axis=1)
                acc = jnp.dot(e_all[u * n_kv + kh], rhs, preferred_element_type=F32)
                o_pairs = (acc[:, :LANES] / (acc[:, LANES:] + t_all[u * n_kv + kh])).astype(BF16)
                for p in range(pairs):
                    col = (kh * pairs + p) * LANES
                    o_ref[rows, col:col + LANES] = o_pairs[p * CHUNK:(p + 1) * CHUNK, :]
        return carry

    def run(masked):
        lax.fori_loop(0, chunks // group, functools.partial(body, masked=masked), 0)

    if hist_chunks >= win_chunks:
        run(False)
    else:
        needs_mask = step * chunks + hist_chunks < win_chunks
        pl.when(needs_mask)(lambda: run(True))
        pl.when(jnp.logical_not(needs_mask))(lambda: run(False))


def _attention(q, ka, kb, va, vb, sinks, *, tq):
    b, s, q_dim = q.shape
    rows, kvw = ka.shape[1:]
    n_kv = kvw // LANES
    span = WINDOW + CHUNK
    chunks = tq // CHUNK
    assert rows >= span and (rows - s) % CHUNK == 0
    hist_spec = pl.BlockSpec((None, rows, kvw), lambda i, c: (i, 0, 0))
    kern = functools.partial(_attn_kernel, n_kv=n_kv, chunks=chunks, hist_chunks=(rows - s) // CHUNK)
    group = 2 if chunks % 2 == 0 else 1
    return pl.pallas_call(
        kern,
        grid=(b, s // tq),
        in_specs=[
            pl.BlockSpec(memory_space=pltpu.SMEM),
            pl.BlockSpec((None, tq, q_dim), lambda i, c: (i, c, 0)),
            hist_spec, hist_spec, hist_spec, hist_spec,
        ],
        out_specs=pl.BlockSpec((None, tq, q_dim), lambda i, c: (i, c, 0)),
        out_shape=jax.ShapeDtypeStruct((b, s, q_dim), BF16),
        scratch_shapes=[pltpu.VMEM((group, n_kv, (GQA_GROUP // 2) * CHUNK, 2 * span), F32)],
        compiler_params=_params("parallel", "arbitrary"),
        name="swa_attention",
    )(sinks, q, ka, kb, va, vb)


def _pair_layouts(x3, n_kv):
    b, r, _ = x3.shape
    x4 = x3.astype(BF16).reshape(b, r, n_kv, HEAD_DIM)
    lo = jnp.pad(x4, ((0, 0), (0, 0), (0, 0), (0, LANES - HEAD_DIM)))
    hi = jnp.pad(x4, ((0, 0), (0, 0), (0, 0), (LANES - HEAD_DIM, 0)))
    return lo.reshape(b, r, n_kv * LANES), hi.reshape(b, r, n_kv * LANES)


def _oproj_kernel(o_ref, x_ref, w_ref, g_ref, y_ref):
    mix = jnp.dot(o_ref[...], w_ref[...], preferred_element_type=F32)
    y_ref[...] = x_ref[...] + _rms(mix, g_ref[...])


def _out_proj(o, x, w_bf16, g, *, tm):
    m, d = x.shape
    return pl.pallas_call(
        _oproj_kernel,
        grid=(m // tm,),
        in_specs=[
            pl.BlockSpec((tm, o.shape[1]), lambda i: (i, 0)),
            pl.BlockSpec((tm, d), lambda i: (i, 0)),
            _resident(w_bf16.shape),
            _resident((1, d)),
        ],
        out_specs=pl.BlockSpec((tm, d), lambda i: (i, 0)),
        out_shape=jax.ShapeDtypeStruct((m, d), F32),
        compiler_params=_params("parallel"),
        name="attn_out_proj",
    )(o, x, w_bf16, g)


def _pool_kernel(x_ref, prev_ref, gpre_ref, gpost_ref, w_ref, scale_ref, y_ref, tail_ref,
                 *, tm, pos0, prev_is_normed, group_dim):
    i = pl.program_id(1)
    gpre = gpre_ref[...]
    x = x_ref[...]
    h = _rms(x, gpre)
    tail_ref[...] = h[tm - HALO:, :]
    if prev_is_normed:
        prev = prev_ref[...]
    else:
        prev = jnp.where(i > 0, _rms(prev_ref[...], gpre), 0.0)
    pos = pos0 + i * tm + lax.broadcasted_iota(jnp.int32, (tm, 1), 0)
    outs = []
    for gi, w in enumerate(POOL_WINDOWS):
        sl = slice(gi * group_dim, (gi + 1) * group_dim)
        hg = h[:, sl]
        acc = jnp.concatenate([prev[:, sl], hg], axis=0)
        shift = 1
        while shift < w:
            acc = acc + pltpu.roll(acc, shift, 0)
            shift *= 2
        cnt = jnp.minimum(pos + 1, w).astype(F32)
        dgrp = (acc[HALO:, :] / cnt - hg).astype(BF16)
        outs.append(jnp.dot(dgrp, w_ref[gi], preferred_element_type=F32))
    mix = jnp.concatenate(outs, axis=1) * scale_ref[...]
    y_ref[...] = x + _rms(mix, gpost_ref[...])


def _pool_mixer(x, prev, gpre, gpost, w_bf16, scale, *, tm, pos0, prev_is_normed):
    b, s, d = x.shape
    nt = s // tm
    group_dim = d // len(POOL_WINDOWS)
    if prev_is_normed:
        prev_spec = pl.BlockSpec((None, HALO, d), lambda bi, i: (bi, 0, 0))
    else:
        per = tm // HALO
        prev_spec = pl.BlockSpec((None, HALO, d), lambda bi, i: (bi, jnp.maximum(i * per - 1, 0), 0))
    kern = functools.partial(_pool_kernel, tm=tm, pos0=pos0, prev_is_normed=prev_is_normed, group_dim=group_dim)
    return pl.pallas_call(
        kern,
        grid=(b, nt),
        in_specs=[
            pl.BlockSpec((None, tm, d), lambda bi, i: (bi, i, 0)),
            prev_spec,
            _resident((1, d)),
            _resident((1, d)),
            _resident(w_bf16.shape),
            _resident((1, d)),
        ],
        out_specs=[
            pl.BlockSpec((None, tm, d), lambda bi, i: (bi, i, 0)),
            pl.BlockSpec((None, None, HALO, d), lambda bi, i: (bi, i, 0, 0)),
        ],
        out_shape=[
            jax.ShapeDtypeStruct((b, s, d), F32),
            jax.ShapeDtypeStruct((b, nt, HALO, d), F32),
        ],
        compiler_params=_params("parallel", "parallel"),
        name="pool_mixer",
    )(x, prev, gpre, gpost, w_bf16, scale)


N_FFN_IN = 8


def _ffn_kernel(*refs, nf):
    x_ref, p_ref, gpre_ref, gpost_ref, wup_ref, wdown_ref, wgate_ref, wproj_ref = refs[:N_FFN_IN]
    n_cast = (len(refs) - N_FFN_IN - 2) // 2
    cast_in = refs[N_FFN_IN:N_FFN_IN + n_cast]
    y_ref = refs[N_FFN_IN + n_cast]
    cast_out = refs[N_FFN_IN + n_cast + 1:N_FFN_IN + 2 * n_cast + 1]
    h_scr = refs[-1]
    _run_casts(cast_in, cast_out)
    f = pl.program_id(1)

    @pl.when(f == 0)
    def _():
        h_scr[...] = _rms(x_ref[...], gpre_ref[...]).astype(BF16)
        y_ref[...] = jnp.zeros_like(y_ref)

    u = jnp.maximum(jnp.dot(h_scr[...], wup_ref[...], preferred_element_type=F32), 0.0)
    y_ref[...] += jnp.dot((u * u).astype(BF16), wdown_ref[...], preferred_element_type=F32)

    @pl.when(f == nf - 1)
    def _():
        x1 = x_ref[...] + _rms(y_ref[...], gpost_ref[...])
        gate = jax.nn.sigmoid(jnp.dot(x1.astype(BF16), wgate_ref[...], preferred_element_type=F32))
        emb = jnp.dot(p_ref[...].astype(BF16), wproj_ref[...], preferred_element_type=F32)
        y_ref[...] = x1 + gate * emb


def _ffn_ple(x, p, gpre, gpost, wup, wdown, wgate, wproj, *, layer, tm, tf, casts=()):
    m, d = x.shape
    dff = wup.shape[1]
    nf = dff // tf
    c_in, c_out, c_shapes, c_ops = _cast_specs(casts, (m // tm) * nf, lambda i, f: i * nf + f)
    outs = pl.pallas_call(
        functools.partial(_ffn_kernel, nf=nf),
        grid=(m // tm, nf),
        in_specs=[
            pl.BlockSpec((tm, d), lambda i, f: (i, 0)),
            pl.BlockSpec((None, tm, p.shape[2]), lambda i, f: (layer, i, 0)),
            _resident((1, d)),
            _resident((1, d)),
            pl.BlockSpec((d, tf), lambda i, f: (0, f)),
            pl.BlockSpec((tf, d), lambda i, f: (f, 0)),
            _resident(wgate.shape),
            _resident(wproj.shape, layer),
        ] + c_in,
        out_specs=[pl.BlockSpec((tm, d), lambda i, f: (i, 0))] + c_out,
        out_shape=[jax.ShapeDtypeStruct((m, d), F32)] + c_shapes,
        scratch_shapes=[pltpu.VMEM((tm, d), BF16)],
        compiler_params=_params("parallel", "arbitrary"),
        name="ffn_ple",
    )(x, p, gpre, gpost, wup, wdown, wgate, wproj, *c_ops)
    return outs


def _row(v):
    return v.reshape(1, -1)


def kernel(x_prompt, x_sample, cache_k, cache_v, state_pool, p_prompt, p_sample, norm_mix_pre, norm_mix_post, norm_ffn_pre, norm_ffn_post, w_qkv, b_qkv, w_o, sinks, w_pool, pool_scale, w_ffn_up, w_ffn_down, w_ple_proj, w_ple_gate):
    b, s, d = x_prompt.shape
    bs, ts, _ = x_sample.shape
    q_dim = w_o.shape[1]
    kv_dim = (w_qkv.shape[2] - q_dim) // 2
    n_kv = kv_dim // HEAD_DIM
    tm = 512
    assert s % tm == 0 and (bs * ts) % tm == 0 and tm % ts == 0

    def bf(w):
        return w.astype(BF16)

    xp = x_prompt.reshape(b * s, d)
    xs = x_sample.reshape(bs * ts, d)
    wproj = bf(w_ple_proj)
    pp = p_prompt.reshape(p_prompt.shape[0], b * s, -1)
    ps = p_sample.reshape(p_sample.shape[0], bs * ts, -1)

    def ffn(x, p, i, weights, casts=()):
        return _ffn_ple(x, p, _row(norm_ffn_pre[i]), _row(norm_ffn_post[i]), *weights, wproj,
                        layer=i, tm=tm, tf=1024, casts=casts)

    tab_p = _rope_tables(jnp.arange(s, dtype=jnp.int32))
    tab_s = tuple(jnp.tile(t, (tm // ts, 1)) for t in _rope_tables(PAST_LEN + jnp.arange(ts, dtype=jnp.int32)))
    wqkv, bqkv = bf(w_qkv[0]), _row(b_qkv[0])
    g_pre, g_post = _row(norm_mix_pre[0]), _row(norm_mix_post[0])

    casts0 = ((w_ffn_up, 0), (w_ffn_down, 0), (w_ple_gate, 0), (w_ple_gate, 1), (w_o, 0))
    q_p, k_p, v_p, *rest = _qkv_rope(xp, g_pre, wqkv, bqkv, tab_p, tm=tm, q_dim=q_dim, kv_dim=kv_dim, casts=casts0)
    kv_p, (wup0, wdown0, wgate0, wgate1, wo) = rest[:4], rest[4:]
    q_s, k_s, v_s, *kv_s = _qkv_rope(xs, g_pre, wqkv, bqkv, tab_s, tm=tm, q_dim=q_dim, kv_dim=kv_dim)

    k_p3, v_p3 = k_p.reshape(b, s, kv_dim), v_p.reshape(b, s, kv_dim)
    k_s3, v_s3 = k_s.reshape(bs, ts, kv_dim), v_s.reshape(bs, ts, kv_dim)
    o_p = _attention(q_p.reshape(b, s, q_dim), *(t.reshape(b, s, -1) for t in kv_p), sinks[0], tq=256)
    cache = (*_pair_layouts(cache_k[0].reshape(bs, WINDOW, kv_dim), n_kv),
             *_pair_layouts(cache_v[0].reshape(bs, WINDOW, kv_dim), n_kv))
    kv_s = (jnp.concatenate([c, t.reshape(bs, ts, -1)], axis=1) for c, t in zip(cache, kv_s))
    o_s = _attention(q_s.reshape(bs, ts, q_dim), *kv_s, sinks[0], tq=ts)

    xp = _out_proj(o_p.reshape(b * s, q_dim), xp, wo, g_post, tm=tm)
    xs = _out_proj(o_s.reshape(bs * ts, q_dim), xs, wo, g_post, tm=tm)

    xp, wup1, wdown1 = ffn(xp, pp, 0, (wup0, wdown0, wgate0), casts=((w_ffn_up, 1), (w_ffn_down, 1)))
    (xs,) = ffn(xs, ps, 0, (wup0, wdown0, wgate0))

    g_pre, g_post = _row(norm_mix_pre[1]), _row(norm_mix_post[1])
    wpool, pscale = bf(w_pool[0]), _row(pool_scale[0])
    xp3, tail_p = _pool_mixer(xp.reshape(b, s, d), xp.reshape(b, s, d), g_pre, g_post, wpool, pscale,
                              tm=256, pos0=0, prev_is_normed=False)
    hist = jnp.pad(state_pool[0], ((0, 0), (HALO - POOL_HIST, 0), (0, 0)))
    xs3, tail_s = _pool_mixer(xs.reshape(bs, ts, d), hist, g_pre, g_post, wpool, pscale,
                              tm=ts, pos0=PAST_LEN, prev_is_normed=True)

    (xp,) = ffn(xp3.reshape(b * s, d), pp, 1, (wup1, wdown1, wgate1))
    (xs,) = ffn(xs3.reshape(bs * ts, d), ps, 1, (wup1, wdown1, wgate1))

    n_heads_kv = (n_kv, HEAD_DIM)
    return (
        xp.reshape(b, s, d),
        xs.reshape(bs, ts, d),
        k_p3[:, s - WINDOW:].reshape(1, b, WINDOW, *n_heads_kv),
        v_p3[:, s - WINDOW:].reshape(1, b, WINDOW, *n_heads_kv),
        tail_p[:, -1, HALO - POOL_HIST:][None],
        k_s3.reshape(1, bs, ts, *n_heads_kv),
        v_s3.reshape(1, bs, ts, *n_heads_kv),
        tail_s[:, -1, HALO - POOL_HIST:][None],
    )
```

```python
import functools
import math

import jax
import jax.numpy as jnp
from jax import lax
from jax.experimental import pallas as pl
from jax.experimental.pallas import tpu as pltpu

HEAD_DIM = 64
GQA_GROUP = 8
CHUNK = 64
WINDOW = 128
ROT_DIM = HEAD_DIM // 4
ROPE_THETA = 500000.0
POOL_WINDOWS = (2, 4, 8, 16)
POOL_HIST = max(POOL_WINDOWS) - 1
PAST_LEN = 2048
EPS = 1e-6

LANES = 128
BF16_ROWS = 16
HALO = 16
VMEM_LIMIT_BYTES = 60 * 1024 * 1024

F32 = jnp.float32
BF16 = jnp.bfloat16


def _rms(x, g):
    ms = jnp.mean(x * x, axis=-1, keepdims=True)
    return x * lax.rsqrt(ms + EPS) * g


def _resident(shape, layer=None):
    if layer is None:
        nd = len(shape)
        return pl.BlockSpec(shape, lambda *_: (0,) * nd, pipeline_mode=pl.Buffered(1))
    nd = len(shape) - 1
    return pl.BlockSpec((None,) + tuple(shape[1:]), lambda *_: (layer,) + (0,) * nd, pipeline_mode=pl.Buffered(1))


def _params(*sem):
    return pltpu.CompilerParams(dimension_semantics=sem, vmem_limit_bytes=VMEM_LIMIT_BYTES)


def _cast_specs(casts, n_steps, flat_step):
    in_specs, out_specs, out_shapes, operands = [], [], [], []
    for w, layer in casts:
        _, r, c = w.shape
        rb = max(BF16_ROWS, r // n_steps)
        csplit = rb * n_steps // r
        assert r % rb == 0 and (r // rb) * csplit == n_steps and c % (csplit * LANES) == 0
        cw = c // csplit

        def in_map(*idx, layer=layer, csplit=csplit):
            k = flat_step(*idx)
            return (layer, k // csplit, k % csplit)

        def out_map(*idx, csplit=csplit):
            k = flat_step(*idx)
            return (k // csplit, k % csplit)

        in_specs.append(pl.BlockSpec((None, rb, cw), in_map))
        out_specs.append(pl.BlockSpec((rb, cw), out_map))
        out_shapes.append(jax.ShapeDtypeStruct((r, c), BF16))
        operands.append(w)
    return in_specs, out_specs, out_shapes, operands


def _run_casts(srcs, dsts):
    for src, dst in zip(srcs, dsts, strict=True):
        dst[...] = src[...].astype(BF16)


def _rope_tables(positions):
    half = ROT_DIM // 2
    inv = ROPE_THETA ** (-jnp.arange(0, ROT_DIM, 2, dtype=F32) / ROT_DIM)
    ang = positions.astype(F32)[:, None] * inv[None, :]
    cos, sin = jnp.cos(ang), jnp.sin(ang)
    p = positions.shape[0]
    ones = jnp.ones((p, HEAD_DIM - ROT_DIM), F32)
    zeros = jnp.zeros((p, HEAD_DIM - ROT_DIM), F32)
    zh = jnp.zeros((p, half), F32)
    c = jnp.concatenate([cos, cos, ones], axis=1)
    sa = jnp.concatenate([-sin, zh, zeros], axis=1)
    sb = jnp.concatenate([zh, sin, zeros], axis=1)
    rep = LANES // HEAD_DIM
    return tuple(jnp.tile(t, (1, rep)) for t in (c, sa, sb))


def _store_pair_layouts(blk, a_ref, b_ref, col):
    lo = lax.broadcasted_iota(jnp.int32, blk.shape, 1) < HEAD_DIM
    swapped = pltpu.roll(blk, HEAD_DIM, 1)
    zero = jnp.zeros_like(blk)
    a_ref[:, col:col + LANES] = jnp.where(lo, blk, zero).astype(BF16)
    b_ref[:, col:col + LANES] = jnp.where(lo, zero, swapped).astype(BF16)
    a_ref[:, col + LANES:col + 2 * LANES] = jnp.where(lo, swapped, zero).astype(BF16)
    b_ref[:, col + LANES:col + 2 * LANES] = jnp.where(lo, zero, blk).astype(BF16)


N_QKV_IN = 7
N_QKV_OUT = 7


def _qkv_kernel(*refs, q_dim, kv_dim, nchunk):
    x_ref, g_ref, w_ref, b_ref, c_ref, sa_ref, sb_ref = refs[:N_QKV_IN]
    n_cast = (len(refs) - N_QKV_IN - N_QKV_OUT) // 2
    cast_in = refs[N_QKV_IN:N_QKV_IN + n_cast]
    q_ref, k_ref, v_ref, ka_ref, kb_ref, va_ref, vb_ref = refs[N_QKV_IN + n_cast:N_QKV_IN + n_cast + N_QKV_OUT]
    cast_out = refs[N_QKV_IN + n_cast + N_QKV_OUT:]
    _run_casts(cast_in, cast_out)

    h = _rms(x_ref[...], g_ref[...]).astype(BF16)
    c, sa, sb = c_ref[...], sa_ref[...], sb_ref[...]
    scale = 1.0 / math.sqrt(HEAD_DIM)
    cq, saq, sbq = c * scale, sa * scale, sb * scale
    half = ROT_DIM // 2
    heads_per_blk = LANES // HEAD_DIM
    rot_cols = q_dim + kv_dim
    total = q_dim + 2 * kv_dim
    for n0 in range(0, total, nchunk):
        acc = jnp.dot(h, w_ref[:, n0:n0 + nchunk], preferred_element_type=F32) + b_ref[:, n0:n0 + nchunk]
        for j0 in range(0, nchunk, LANES):
            col = n0 + j0
            blk = acc[:, j0:j0 + LANES]
            if col < q_dim:
                blk = blk * cq + pltpu.roll(blk, LANES - half, 1) * saq + pltpu.roll(blk, half, 1) * sbq
                q_ref[:, col:col + LANES] = blk.astype(BF16)
            elif col < rot_cols:
                blk = blk * c + pltpu.roll(blk, LANES - half, 1) * sa + pltpu.roll(blk, half, 1) * sb
                k_ref[:, col - q_dim:col - q_dim + LANES] = blk
                _store_pair_layouts(blk, ka_ref, kb_ref, (col - q_dim) * heads_per_blk)
            else:
                v_ref[:, col - rot_cols:col - rot_cols + LANES] = blk
                _store_pair_layouts(blk, va_ref, vb_ref, (col - rot_cols) * heads_per_blk)


def _qkv_rope(x, g, w_bf16, b, tables, *, tm, q_dim, kv_dim, casts=()):
    m, d = x.shape
    total = q_dim + 2 * kv_dim
    ptiles = tables[0].shape[0] // tm
    tab_spec = pl.BlockSpec((tm, LANES), lambda i: (i % ptiles, 0))
    kern = functools.partial(_qkv_kernel, q_dim=q_dim, kv_dim=kv_dim, nchunk=256)
    pair_w = kv_dim * (LANES // HEAD_DIM)
    c_in, c_out, c_shapes, c_ops = _cast_specs(casts, m // tm, lambda i: i)

    def rows(width):
        return pl.BlockSpec((tm, width), lambda i: (i, 0))

    return pl.pallas_call(
        kern,
        grid=(m // tm,),
        in_specs=[
            rows(d),
            _resident((1, d)),
            _resident((d, total)),
            _resident((1, total)),
            tab_spec, tab_spec, tab_spec,
        ] + c_in,
        out_specs=[rows(q_dim), rows(kv_dim), rows(kv_dim), rows(pair_w), rows(pair_w), rows(pair_w), rows(pair_w)]
        + c_out,
        out_shape=[
            jax.ShapeDtypeStruct((m, q_dim), BF16),
            jax.ShapeDtypeStruct((m, kv_dim), F32),
            jax.ShapeDtypeStruct((m, kv_dim), F32),
        ] + [jax.ShapeDtypeStruct((m, pair_w), BF16)] * 4 + c_shapes,
        compiler_params=_params("parallel"),
        name="qkv_rope",
    )(x, g, w_bf16, b, *tables, *c_ops)


def _chunks_per_body(chunks):
    return next(g for g in (2, 1) if chunks % g == 0)


def _attn_kernel(sink_ref, q_ref, ka_ref, kb_ref, va_ref, vb_ref, o_ref, s_scr, *, n_kv, chunks, group, hist_chunks):
    step = pl.program_id(1)
    span = WINDOW + CHUNK
    pairs = GQA_GROUP // 2
    win_chunks = WINDOW // CHUNK
    neg_inf = -jnp.inf
    row_k = lax.broadcasted_iota(jnp.int32, (2 * span, LANES), 0)
    lane_k = lax.broadcasted_iota(jnp.int32, (2 * span, LANES), 1)
    ones_ab = jnp.where((row_k < span) == (lane_k < HEAD_DIM), 1.0, 0.0).astype(BF16)
    lo = lax.broadcasted_iota(jnp.int32, (CHUNK, LANES), 1) < HEAD_DIM
    nt = (((1,), (1,)), ((), ()))

    def body(j, carry, *, masked):
        units = []
        for u in range(group):
            ci = j * group + u
            ckv = step * chunks + ci + hist_chunks
            k0 = pl.multiple_of(jnp.maximum(ckv - win_chunks, 0) * CHUNK, CHUNK)
            units.append((pl.ds(pl.multiple_of(ci * CHUNK, CHUNK), CHUNK), pl.ds(k0, span), ckv))

        for u, (rows, win, _) in enumerate(units):
            for kh in range(n_kv):
                kcols = slice(kh * LANES, (kh + 1) * LANES)
                q_pairs = jnp.concatenate(
                    [q_ref[rows, (kh * pairs + p) * LANES:(kh * pairs + p + 1) * LANES] for p in range(pairs)], axis=0)
                k_ab = jnp.concatenate([ka_ref[win, kcols], kb_ref[win, kcols]], axis=0)
                s_scr[u, kh] = lax.dot_general(q_pairs, k_ab, nt, preferred_element_type=F32)

        e_all, t_all = [], []
        for u, (_, _, ckv) in enumerate(units):
            if masked:
                key_row = lax.broadcasted_iota(jnp.int32, (1, span), 1)
                bias1 = jnp.where(key_row < (ckv + 1) * CHUNK, 0.0, neg_inf)
                bias = jnp.concatenate([bias1, bias1], axis=1)
            for kh in range(n_kv):
                e_rows, t_rows = [], []
                for p in range(pairs):
                    s = s_scr[u, kh, p * CHUNK:(p + 1) * CHUNK, :]
                    if masked:
                        s = s + bias
                    t0, t1, t2 = s[:, :LANES], s[:, LANES:2 * LANES], s[:, 2 * LANES:]
                    sk_a = sink_ref[kh * GQA_GROUP + 2 * p]
                    sk_b = sink_ref[kh * GQA_GROUP + 2 * p + 1]
                    m_a = jnp.max(jnp.maximum(t0, jnp.where(lo, t1, neg_inf)), axis=-1, keepdims=True)
                    m_b = jnp.max(jnp.maximum(t2, jnp.where(lo, neg_inf, t1)), axis=-1, keepdims=True)
                    m_a, m_b = jnp.maximum(m_a, sk_a), jnp.maximum(m_b, sk_b)
                    e = jnp.concatenate(
                        [jnp.exp(t0 - m_a), jnp.exp(t1 - jnp.where(lo, m_a, m_b)), jnp.exp(t2 - m_b)], axis=1)
                    e_rows.append(e.astype(BF16))
                    t_rows.append(jnp.where(lo, jnp.exp(sk_a - m_a), jnp.exp(sk_b - m_b)))
                e_all.append(jnp.concatenate(e_rows, axis=0))
                t_all.append(jnp.concatenate(t_rows, axis=0))

        for u, (rows, win, _) in enumerate(units):
            for kh in range(n_kv):
                kcols = slice(kh * LANES, (kh + 1) * LANES)
                v_ab = jnp.concatenate([va_ref[win, kcols], vb_ref[win, kcols]], axis=0)
                rhs = jnp.concatenate([v_ab, ones_ab], axis=1)
                acc = jnp.dot(e_all[u * n_kv + kh], rhs, preferred_element_type=F32)
                o_pairs = (acc[:, :LANES] / (acc[:, LANES:] + t_all[u * n_kv + kh])).astype(BF16)
                for p in range(pairs):
                    col = (kh * pairs + p) * LANES
                    o_ref[rows, col:col + LANES] = o_pairs[p * CHUNK:(p + 1) * CHUNK, :]
        return carry

    def run(masked):
        lax.fori_loop(0, chunks // group, functools.partial(body, masked=masked), 0)

    if hist_chunks >= win_chunks:
        run(False)
    else:
        needs_mask = step * chunks + hist_chunks < win_chunks
        pl.when(needs_mask)(lambda: run(True))
        pl.when(jnp.logical_not(needs_mask))(lambda: run(False))


def _attention(q, ka, kb, va, vb, sinks, *, tq):
    b, s, q_dim = q.shape
    rows, kvw = ka.shape[1:]
    n_kv = kvw // LANES
    span = WINDOW + CHUNK
    chunks = tq // CHUNK
    assert rows >= span and (rows - s) % CHUNK == 0
    hist_spec = pl.BlockSpec((None, rows, kvw), lambda i, c: (i, 0, 0))
    group = _chunks_per_body(chunks)
    kern = functools.partial(_attn_kernel, n_kv=n_kv, chunks=chunks, group=group, hist_chunks=(rows - s) // CHUNK)
    return pl.pallas_call(
        kern,
        grid=(b, s // tq),
        in_specs=[
            pl.BlockSpec(memory_space=pltpu.SMEM),
            pl.BlockSpec((None, tq, q_dim), lambda i, c: (i, c, 0)),
            hist_spec, hist_spec, hist_spec, hist_spec,
        ],
        out_specs=pl.BlockSpec((None, tq, q_dim), lambda i, c: (i, c, 0)),
        out_shape=jax.ShapeDtypeStruct((b, s, q_dim), BF16),
        scratch_shapes=[pltpu.VMEM((group, n_kv, (GQA_GROUP // 2) * CHUNK, 2 * span), F32)],
        compiler_params=_params("parallel", "arbitrary"),
        name="swa_attention",
    )(sinks, q, ka, kb, va, vb)


def _pair_layouts(x3, n_kv):
    b, r, _ = x3.shape
    x4 = x3.astype(BF16).reshape(b, r, n_kv, HEAD_DIM)
    lo = jnp.pad(x4, ((0, 0), (0, 0), (0, 0), (0, LANES - HEAD_DIM)))
    hi = jnp.pad(x4, ((0, 0), (0, 0), (0, 0), (LANES - HEAD_DIM, 0)))
    return lo.reshape(b, r, n_kv * LANES), hi.reshape(b, r, n_kv * LANES)


def _oproj_kernel(o_ref, x_ref, w_ref, g_ref, y_ref):
    mix = jnp.dot(o_ref[...], w_ref[...], preferred_element_type=F32)
    y_ref[...] = x_ref[...] + _rms(mix, g_ref[...])


def _out_proj(o, x, w_bf16, g, *, tm):
    m, d = x.shape
    return pl.pallas_call(
        _oproj_kernel,
        grid=(m // tm,),
        in_specs=[
            pl.BlockSpec((tm, o.shape[1]), lambda i: (i, 0)),
            pl.BlockSpec((tm, d), lambda i: (i, 0)),
            _resident(w_bf16.shape),
            _resident((1, d)),
        ],
        out_specs=pl.BlockSpec((tm, d), lambda i: (i, 0)),
        out_shape=jax.ShapeDtypeStruct((m, d), F32),
        compiler_params=_params("parallel"),
        name="attn_out_proj",
    )(o, x, w_bf16, g)


def _pool_kernel(x_ref, prev_ref, gpre_ref, gpost_ref, w_ref, scale_ref, y_ref, tail_ref,
                 *, tm, pos0, prev_is_normed, group_dim):
    i = pl.program_id(1)
    gpre = gpre_ref[...]
    x = x_ref[...]
    h = _rms(x, gpre)
    tail_ref[...] = h[tm - HALO:, :]
    if prev_is_normed:
        prev = prev_ref[...]
    else:
        prev = jnp.where(i > 0, _rms(prev_ref[...], gpre), 0.0)
    pos = pos0 + i * tm + lax.broadcasted_iota(jnp.int32, (tm, 1), 0)
    outs = []
    for gi, w in enumerate(POOL_WINDOWS):
        sl = slice(gi * group_dim, (gi + 1) * group_dim)
        hg = h[:, sl]
        acc = jnp.concatenate([prev[:, sl], hg], axis=0)
        shift = 1
        while shift < w:
            acc = acc + pltpu.roll(acc, shift, 0)
            shift *= 2
        cnt = jnp.minimum(pos + 1, w).astype(F32)
        dgrp = (acc[HALO:, :] / cnt - hg).astype(BF16)
        outs.append(jnp.dot(dgrp, w_ref[gi], preferred_element_type=F32))
    mix = jnp.concatenate(outs, axis=1) * scale_ref[...]
    y_ref[...] = x + _rms(mix, gpost_ref[...])


def _pool_mixer(x, prev, gpre, gpost, w_bf16, scale, *, tm, pos0, prev_is_normed):
    b, s, d = x.shape
    nt = s // tm
    group_dim = d // len(POOL_WINDOWS)
    if prev_is_normed:
        prev_spec = pl.BlockSpec((None, HALO, d), lambda bi, i: (bi, 0, 0))
    else:
        per = tm // HALO
        prev_spec = pl.BlockSpec((None, HALO, d), lambda bi, i: (bi, jnp.maximum(i * per - 1, 0), 0))
    kern = functools.partial(_pool_kernel, tm=tm, pos0=pos0, prev_is_normed=prev_is_normed, group_dim=group_dim)
    return pl.pallas_call(
        kern,
        grid=(b, nt),
        in_specs=[
            pl.BlockSpec((None, tm, d), lambda bi, i: (bi, i, 0)),
            prev_spec,
            _resident((1, d)),
            _resident((1, d)),
            _resident(w_bf16.shape),
            _resident((1, d)),
        ],
        out_specs=[
            pl.BlockSpec((None, tm, d), lambda bi, i: (bi, i, 0)),
            pl.BlockSpec((None, None, HALO, d), lambda bi, i: (bi, i, 0, 0)),
        ],
        out_shape=[
            jax.ShapeDtypeStruct((b, s, d), F32),
            jax.ShapeDtypeStruct((b, nt, HALO, d), F32),
        ],
        compiler_params=_params("parallel", "parallel"),
        name="pool_mixer",
    )(x, prev, gpre, gpost, w_bf16, scale)


N_FFN_IN = 8


def _ffn_kernel(*refs, nf):
    x_ref, p_ref, gpre_ref, gpost_ref, wup_ref, wdown_ref, wgate_ref, wproj_ref = refs[:N_FFN_IN]
    n_cast = (len(refs) - N_FFN_IN - 2) // 2
    cast_in = refs[N_FFN_IN:N_FFN_IN + n_cast]
    y_ref = refs[N_FFN_IN + n_cast]
    cast_out = refs[N_FFN_IN + n_cast + 1:N_FFN_IN + 2 * n_cast + 1]
    h_scr = refs[-1]
    _run_casts(cast_in, cast_out)
    f = pl.program_id(1)

    def mlp_part(h):
        u = jnp.maximum(jnp.dot(h, wup_ref[...], preferred_element_type=F32), 0.0)
        return jnp.dot((u * u).astype(BF16), wdown_ref[...], preferred_element_type=F32)

    @pl.when(f == 0)
    def _():
        h = _rms(x_ref[...], gpre_ref[...]).astype(BF16)
        h_scr[...] = h
        y_ref[...] = mlp_part(h)

    @pl.when(jnp.logical_and(f > 0, f < nf - 1))
    def _():
        y_ref[...] += mlp_part(h_scr[...])

    @pl.when(f == nf - 1)
    def _():
        emb = jnp.dot(p_ref[...].astype(BF16), wproj_ref[...], preferred_element_type=F32)
        x1 = x_ref[...] + _rms(y_ref[...] + mlp_part(h_scr[...]), gpost_ref[...])
        gate = jax.nn.sigmoid(jnp.dot(x1.astype(BF16), wgate_ref[...], preferred_element_type=F32))
        y_ref[...] = x1 + gate * emb


def _ffn_ple(x, p, gpre, gpost, wup, wdown, wgate, wproj, *, layer, tm, tf, casts=()):
    m, d = x.shape
    dff = wup.shape[1]
    nf = dff // tf
    assert nf >= 2
    c_in, c_out, c_shapes, c_ops = _cast_specs(casts, (m // tm) * nf, lambda i, f: i * nf + f)
    outs = pl.pallas_call(
        functools.partial(_ffn_kernel, nf=nf),
        grid=(m // tm, nf),
        in_specs=[
            pl.BlockSpec((tm, d), lambda i, f: (i, 0)),
            pl.BlockSpec((None, tm, p.shape[2]), lambda i, f: (layer, i, 0)),
            _resident((1, d)),
            _resident((1, d)),
            pl.BlockSpec((d, tf), lambda i, f: (0, f)),
            pl.BlockSpec((tf, d), lambda i, f: (f, 0)),
            _resident(wgate.shape),
            _resident(wproj.shape, layer),
        ] + c_in,
        out_specs=[pl.BlockSpec((tm, d), lambda i, f: (i, 0))] + c_out,
        out_shape=[jax.ShapeDtypeStruct((m, d), F32)] + c_shapes,
        scratch_shapes=[pltpu.VMEM((tm, d), BF16)],
        compiler_params=_params("parallel", "arbitrary"),
        name="ffn_ple",
    )(x, p, gpre, gpost, wup, wdown, wgate, wproj, *c_ops)
    return outs


def _row(v):
    return v.reshape(1, -1)


def kernel(x_prompt, x_sample, cache_k, cache_v, state_pool, p_prompt, p_sample, norm_mix_pre, norm_mix_post, norm_ffn_pre, norm_ffn_post, w_qkv, b_qkv, w_o, sinks, w_pool, pool_scale, w_ffn_up, w_ffn_down, w_ple_proj, w_ple_gate):
    b, s, d = x_prompt.shape
    bs, ts, _ = x_sample.shape
    q_dim = w_o.shape[1]
    kv_dim = (w_qkv.shape[2] - q_dim) // 2
    n_kv = kv_dim // HEAD_DIM
    tm = 512
    assert s % tm == 0 and (bs * ts) % tm == 0 and tm % ts == 0

    def bf(w):
        return w.astype(BF16)

    xp = x_prompt.reshape(b * s, d)
    xs = x_sample.reshape(bs * ts, d)
    wproj = bf(w_ple_proj)
    pp = p_prompt.reshape(p_prompt.shape[0], b * s, -1)
    ps = p_sample.reshape(p_sample.shape[0], bs * ts, -1)

    def ffn(x, p, i, weights, casts=()):
        return _ffn_ple(x, p, _row(norm_ffn_pre[i]), _row(norm_ffn_post[i]), *weights, wproj,
                        layer=i, tm=tm, tf=1024, casts=casts)

    tab_p = _rope_tables(jnp.arange(s, dtype=jnp.int32))
    tab_s = tuple(jnp.tile(t, (tm // ts, 1)) for t in _rope_tables(PAST_LEN + jnp.arange(ts, dtype=jnp.int32)))
    wqkv, bqkv = bf(w_qkv[0]), _row(b_qkv[0])
    g_pre, g_post = _row(norm_mix_pre[0]), _row(norm_mix_post[0])

    casts0 = ((w_ffn_up, 0), (w_ffn_down, 0), (w_ple_gate, 0), (w_ple_gate, 1), (w_o, 0))
    q_p, k_p, v_p, *rest = _qkv_rope(xp, g_pre, wqkv, bqkv, tab_p, tm=tm, q_dim=q_dim, kv_dim=kv_dim, casts=casts0)
    kv_p, (wup0, wdown0, wgate0, wgate1, wo) = rest[:4], rest[4:]
    q_s, k_s, v_s, *kv_s = _qkv_rope(xs, g_pre, wqkv, bqkv, tab_s, tm=tm, q_dim=q_dim, kv_dim=kv_dim)

    k_p3, v_p3 = k_p.reshape(b, s, kv_dim), v_p.reshape(b, s, kv_dim)
    k_s3, v_s3 = k_s.reshape(bs, ts, kv_dim), v_s.reshape(bs, ts, kv_dim)
    o_p = _attention(q_p.reshape(b, s, q_dim), *(t.reshape(b, s, -1) for t in kv_p), sinks[0], tq=512)
    cache = (*_pair_layouts(cache_k[0].reshape(bs, WINDOW, kv_dim), n_kv),
             *_pair_layouts(cache_v[0].reshape(bs, WINDOW, kv_dim), n_kv))
    kv_s = (jnp.concatenate([c, t.reshape(bs, ts, -1)], axis=1) for c, t in zip(cache, kv_s))
    o_s = _attention(q_s.reshape(bs, ts, q_dim), *kv_s, sinks[0], tq=ts)

    xp = _out_proj(o_p.reshape(b * s, q_dim), xp, wo, g_post, tm=2 * tm)
    xs = _out_proj(o_s.reshape(bs * ts, q_dim), xs, wo, g_post, tm=tm)

    xp, wup1, wdown1 = ffn(xp, pp, 0, (wup0, wdown0, wgate0), casts=((w_ffn_up, 1), (w_ffn_down, 1)))
    (xs,) = ffn(xs, ps, 0, (wup0, wdown0, wgate0))

    g_pre, g_post = _row(norm_mix_pre[1]), _row(norm_mix_post[1])
    wpool, pscale = bf(w_pool[0]), _row(pool_scale[0])
    xp3, tail_p = _pool_mixer(xp.reshape(b, s, d), xp.reshape(b, s, d), g_pre, g_post, wpool, pscale,
                              tm=tm, pos0=0, prev_is_normed=False)
    hist = jnp.pad(state_pool[0], ((0, 0), (HALO - POOL_HIST, 0), (0, 0)))
    xs3, tail_s = _pool_mixer(xs.reshape(bs, ts, d), hist, g_pre, g_post, wpool, pscale,
                              tm=ts, pos0=PAST_LEN, prev_is_normed=True)

    (xp,) = ffn(xp3.reshape(b * s, d), pp, 1, (wup1, wdown1, wgate1))
    (xs,) = ffn(xs3.reshape(bs * ts, d), ps, 1, (wup1, wdown1, wgate1))

    n_heads_kv = (n_kv, HEAD_DIM)
    return (
        xp.reshape(b, s, d),
        xs.reshape(bs, ts, d),
        k_p3[:, s - WINDOW:].reshape(1, b, WINDOW, *n_heads_kv),
        v_p3[:, s - WINDOW:].reshape(1, b, WINDOW, *n_heads_kv),
        tail_p[:, -1, HALO - POOL_HIST:][None],
        k_s3.reshape(1, bs, ts, *n_heads_kv),
        v_s3.reshape(1, bs, ts, *n_heads_kv),
        tail_s[:, -1, HALO - POOL_HIST:][None],
    )
```

```python
import functools
import math

import jax
import jax.numpy as jnp
from jax import lax
from jax.experimental import pallas as pl
from jax.experimental.pallas import tpu as pltpu

HEAD_DIM = 64
GQA_GROUP = 8
CHUNK = 64
WINDOW = 128
ROT_DIM = HEAD_DIM // 4
ROPE_THETA = 500000.0
POOL_WINDOWS = (2, 4, 8, 16)
POOL_HIST = max(POOL_WINDOWS) - 1
PAST_LEN = 2048
EPS = 1e-6

LANES = 128
BF16_ROWS = 16
HALO = 16
VMEM_LIMIT_BYTES = 60 * 1024 * 1024

F32 = jnp.float32
BF16 = jnp.bfloat16


def _rms(x, g):
    ms = jnp.mean(x * x, axis=-1, keepdims=True)
    return x * lax.rsqrt(ms + EPS) * g


def _resident(shape, layer=None):
    if layer is None:
        nd = len(shape)
        return pl.BlockSpec(shape, lambda *_: (0,) * nd, pipeline_mode=pl.Buffered(1))
    nd = len(shape) - 1
    return pl.BlockSpec((None,) + tuple(shape[1:]), lambda *_: (layer,) + (0,) * nd, pipeline_mode=pl.Buffered(1))


def _params(*sem):
    return pltpu.CompilerParams(dimension_semantics=sem, vmem_limit_bytes=VMEM_LIMIT_BYTES)


def _cast_specs(casts, n_steps, flat_step):
    in_specs, out_specs, out_shapes, operands = [], [], [], []
    for w, layer in casts:
        _, r, c = w.shape
        rb = max(BF16_ROWS, r // n_steps)
        csplit = rb * n_steps // r
        assert r % rb == 0 and (r // rb) * csplit == n_steps and c % (csplit * LANES) == 0
        cw = c // csplit

        def in_map(*idx, layer=layer, csplit=csplit):
            k = flat_step(*idx)
            return (layer, k // csplit, k % csplit)

        def out_map(*idx, csplit=csplit):
            k = flat_step(*idx)
            return (k // csplit, k % csplit)

        in_specs.append(pl.BlockSpec((None, rb, cw), in_map))
        out_specs.append(pl.BlockSpec((rb, cw), out_map))
        out_shapes.append(jax.ShapeDtypeStruct((r, c), BF16))
        operands.append(w)
    return in_specs, out_specs, out_shapes, operands


def _run_casts(srcs, dsts):
    for src, dst in zip(srcs, dsts, strict=True):
        dst[...] = src[...].astype(BF16)


def _rope_tables(positions):
    half = ROT_DIM // 2
    inv = ROPE_THETA ** (-jnp.arange(0, ROT_DIM, 2, dtype=F32) / ROT_DIM)
    ang = positions.astype(F32)[:, None] * inv[None, :]
    cos, sin = jnp.cos(ang), jnp.sin(ang)
    p = positions.shape[0]
    ones = jnp.ones((p, HEAD_DIM - ROT_DIM), F32)
    zeros = jnp.zeros((p, HEAD_DIM - ROT_DIM), F32)
    zh = jnp.zeros((p, half), F32)
    c = jnp.concatenate([cos, cos, ones], axis=1)
    sa = jnp.concatenate([-sin, zh, zeros], axis=1)
    sb = jnp.concatenate([zh, sin, zeros], axis=1)
    rep = LANES // HEAD_DIM
    return tuple(jnp.tile(t, (1, rep)) for t in (c, sa, sb))


def _store_pair_layouts(blk, a_ref, b_ref, col):
    lo = lax.broadcasted_iota(jnp.int32, blk.shape, 1) < HEAD_DIM
    swapped = pltpu.roll(blk, HEAD_DIM, 1)
    zero = jnp.zeros_like(blk)
    a_ref[:, col:col + LANES] = jnp.where(lo, blk, zero).astype(BF16)
    b_ref[:, col:col + LANES] = jnp.where(lo, zero, swapped).astype(BF16)
    a_ref[:, col + LANES:col + 2 * LANES] = jnp.where(lo, swapped, zero).astype(BF16)
    b_ref[:, col + LANES:col + 2 * LANES] = jnp.where(lo, zero, blk).astype(BF16)


N_QKV_IN = 7
N_QKV_OUT = 7


def _qkv_kernel(*refs, q_dim, kv_dim, nchunk):
    x_ref, g_ref, w_ref, b_ref, c_ref, sa_ref, sb_ref = refs[:N_QKV_IN]
    n_cast = (len(refs) - N_QKV_IN - N_QKV_OUT) // 2
    cast_in = refs[N_QKV_IN:N_QKV_IN + n_cast]
    q_ref, k_ref, v_ref, ka_ref, kb_ref, va_ref, vb_ref = refs[N_QKV_IN + n_cast:N_QKV_IN + n_cast + N_QKV_OUT]
    cast_out = refs[N_QKV_IN + n_cast + N_QKV_OUT:]
    _run_casts(cast_in, cast_out)

    h = _rms(x_ref[...], g_ref[...]).astype(BF16)
    c, sa, sb = c_ref[...], sa_ref[...], sb_ref[...]
    scale = 1.0 / math.sqrt(HEAD_DIM)
    cq, saq, sbq = c * scale, sa * scale, sb * scale
    half = ROT_DIM // 2
    heads_per_blk = LANES // HEAD_DIM
    rot_cols = q_dim + kv_dim
    total = q_dim + 2 * kv_dim
    for n0 in range(0, total, nchunk):
        acc = jnp.dot(h, w_ref[:, n0:n0 + nchunk], preferred_element_type=F32) + b_ref[:, n0:n0 + nchunk]
        for j0 in range(0, nchunk, LANES):
            col = n0 + j0
            blk = acc[:, j0:j0 + LANES]
            if col < q_dim:
                blk = blk * cq + pltpu.roll(blk, LANES - half, 1) * saq + pltpu.roll(blk, half, 1) * sbq
                q_ref[:, col:col + LANES] = blk.astype(BF16)
            elif col < rot_cols:
                blk = blk * c + pltpu.roll(blk, LANES - half, 1) * sa + pltpu.roll(blk, half, 1) * sb
                k_ref[:, col - q_dim:col - q_dim + LANES] = blk
                _store_pair_layouts(blk, ka_ref, kb_ref, (col - q_dim) * heads_per_blk)
            else:
                v_ref[:, col - rot_cols:col - rot_cols + LANES] = blk
                _store_pair_layouts(blk, va_ref, vb_ref, (col - rot_cols) * heads_per_blk)


def _qkv_rope(x, g, w_bf16, b, tables, *, tm, q_dim, kv_dim, casts=()):
    m, d = x.shape
    total = q_dim + 2 * kv_dim
    ptiles = tables[0].shape[0] // tm
    tab_spec = pl.BlockSpec((tm, LANES), lambda i: (i % ptiles, 0))
    kern = functools.partial(_qkv_kernel, q_dim=q_dim, kv_dim=kv_dim, nchunk=256)
    pair_w = kv_dim * (LANES // HEAD_DIM)
    c_in, c_out, c_shapes, c_ops = _cast_specs(casts, m // tm, lambda i: i)

    def rows(width):
        return pl.BlockSpec((tm, width), lambda i: (i, 0))

    return pl.pallas_call(
        kern,
        grid=(m // tm,),
        in_specs=[
            rows(d),
            _resident((1, d)),
            _resident((d, total)),
            _resident((1, total)),
            tab_spec, tab_spec, tab_spec,
        ] + c_in,
        out_specs=[rows(q_dim), rows(kv_dim), rows(kv_dim), rows(pair_w), rows(pair_w), rows(pair_w), rows(pair_w)]
        + c_out,
        out_shape=[
            jax.ShapeDtypeStruct((m, q_dim), BF16),
            jax.ShapeDtypeStruct((m, kv_dim), F32),
            jax.ShapeDtypeStruct((m, kv_dim), F32),
        ] + [jax.ShapeDtypeStruct((m, pair_w), BF16)] * 4 + c_shapes,
        compiler_params=_params("parallel"),
        name="qkv_rope",
    )(x, g, w_bf16, b, *tables, *c_ops)


def _chunks_per_body(chunks):
    return next(g for g in (2, 1) if chunks % g == 0)


def _attn_kernel(sink_ref, q_ref, ka_ref, kb_ref, va_ref, vb_ref, o_ref, s_scr, *, n_kv, chunks, group, hist_chunks):
    step = pl.program_id(1)
    span = WINDOW + CHUNK
    pairs = GQA_GROUP // 2
    win_chunks = WINDOW // CHUNK
    neg_inf = -jnp.inf
    row_k = lax.broadcasted_iota(jnp.int32, (2 * span, LANES), 0)
    lane_k = lax.broadcasted_iota(jnp.int32, (2 * span, LANES), 1)
    ones_ab = jnp.where((row_k < span) == (lane_k < HEAD_DIM), 1.0, 0.0).astype(BF16)
    lo = lax.broadcasted_iota(jnp.int32, (CHUNK, LANES), 1) < HEAD_DIM
    nt = (((1,), (1,)), ((), ()))

    def body(j, carry, *, masked):
        units = []
        for u in range(group):
            ci = j * group + u
            ckv = step * chunks + ci + hist_chunks
            k0 = pl.multiple_of(jnp.maximum(ckv - win_chunks, 0) * CHUNK, CHUNK)
            units.append((pl.ds(pl.multiple_of(ci * CHUNK, CHUNK), CHUNK), pl.ds(k0, span), ckv))

        for u, (rows, win, _) in enumerate(units):
            for kh in range(n_kv):
                kcols = slice(kh * LANES, (kh + 1) * LANES)
                q_pairs = jnp.concatenate(
                    [q_ref[rows, (kh * pairs + p) * LANES:(kh * pairs + p + 1) * LANES] for p in range(pairs)], axis=0)
                k_ab = jnp.concatenate([ka_ref[win, kcols], kb_ref[win, kcols]], axis=0)
                s_scr[u, kh] = lax.dot_general(q_pairs, k_ab, nt, preferred_element_type=F32)

        e_all, t_all = [], []
        for u, (_, _, ckv) in enumerate(units):
            if masked:
                key_row = lax.broadcasted_iota(jnp.int32, (1, span), 1)
                bias1 = jnp.where(key_row < (ckv + 1) * CHUNK, 0.0, neg_inf)
                bias = jnp.concatenate([bias1, bias1], axis=1)
            for kh in range(n_kv):
                e_rows, t_rows = [], []
                for p in range(pairs):
                    s = s_scr[u, kh, p * CHUNK:(p + 1) * CHUNK, :]
                    if masked:
                        s = s + bias
                    t0, t1, t2 = s[:, :LANES], s[:, LANES:2 * LANES], s[:, 2 * LANES:]
                    sk_a = sink_ref[kh * GQA_GROUP + 2 * p]
                    sk_b = sink_ref[kh * GQA_GROUP + 2 * p + 1]
                    m_a = jnp.max(jnp.maximum(t0, jnp.where(lo, t1, neg_inf)), axis=-1, keepdims=True)
                    m_b = jnp.max(jnp.maximum(t2, jnp.where(lo, neg_inf, t1)), axis=-1, keepdims=True)
                    m_a, m_b = jnp.maximum(m_a, sk_a), jnp.maximum(m_b, sk_b)
                    e = jnp.concatenate(
                        [jnp.exp(t0 - m_a), jnp.exp(t1 - jnp.where(lo, m_a, m_b)), jnp.exp(t2 - m_b)], axis=1)
                    e_rows.append(e.astype(BF16))
                    t_rows.append(jnp.where(lo, jnp.exp(sk_a - m_a), jnp.exp(sk_b - m_b)))
                e_all.append(jnp.concatenate(e_rows, axis=0))
                t_all.append(jnp.concatenate(t_rows, axis=0))

        for u, (rows, win, _) in enumerate(units):
            for kh in range(n_kv):
                kcols = slice(kh * LANES, (kh + 1) * LANES)
                v_ab = jnp.concatenate([va_ref[win, kcols], vb_ref[win, kcols]], axis=0)
                rhs = jnp.concatenate([v_ab, ones_ab], axis=1)
                acc = jnp.dot(e_all[u * n_kv + kh], rhs, preferred_element_type=F32)
                o_pairs = (acc[:, :LANES] / (acc[:, LANES:] + t_all[u * n_kv + kh])).astype(BF16)
                for p in range(pairs):
                    col = (kh * pairs + p) * LANES
                    o_ref[rows, col:col + LANES] = o_pairs[p * CHUNK:(p + 1) * CHUNK, :]
        return carry

    def run(masked):
        lax.fori_loop(0, chunks // group, functools.partial(body, masked=masked), 0)

    if hist_chunks >= win_chunks:
        run(False)
    else:
        needs_mask = step * chunks + hist_chunks < win_chunks
        pl.when(needs_mask)(lambda: run(True))
        pl.when(jnp.logical_not(needs_mask))(lambda: run(False))


def _attention(q, ka, kb, va, vb, sinks, *, tq):
    b, s, q_dim = q.shape
    rows, kvw = ka.shape[1:]
    n_kv = kvw // LANES
    span = WINDOW + CHUNK
    chunks = tq // CHUNK
    assert rows >= span and (rows - s) % CHUNK == 0
    hist_spec = pl.BlockSpec((None, rows, kvw), lambda i, c: (i, 0, 0))
    group = _chunks_per_body(chunks)
    kern = functools.partial(_attn_kernel, n_kv=n_kv, chunks=chunks, group=group, hist_chunks=(rows - s) // CHUNK)
    return pl.pallas_call(
        kern,
        grid=(b, s // tq),
        in_specs=[
            pl.BlockSpec(memory_space=pltpu.SMEM),
            pl.BlockSpec((None, tq, q_dim), lambda i, c: (i, c, 0)),
            hist_spec, hist_spec, hist_spec, hist_spec,
        ],
        out_specs=pl.BlockSpec((None, tq, q_dim), lambda i, c: (i, c, 0)),
        out_shape=jax.ShapeDtypeStruct((b, s, q_dim), BF16),
        scratch_shapes=[pltpu.VMEM((group, n_kv, (GQA_GROUP // 2) * CHUNK, 2 * span), F32)],
        compiler_params=_params("parallel", "arbitrary"),
        name="swa_attention",
    )(sinks, q, ka, kb, va, vb)


def _pair_layouts(x3, n_kv):
    b, r, _ = x3.shape
    x4 = x3.astype(BF16).reshape(b, r, n_kv, HEAD_DIM)
    lo = jnp.pad(x4, ((0, 0), (0, 0), (0, 0), (0, LANES - HEAD_DIM)))
    hi = jnp.pad(x4, ((0, 0), (0, 0), (0, 0), (LANES - HEAD_DIM, 0)))
    return lo.reshape(b, r, n_kv * LANES), hi.reshape(b, r, n_kv * LANES)


def _oproj_kernel(o_ref, x_ref, w_ref, g_ref, y_ref):
    mix = jnp.dot(o_ref[...], w_ref[...], preferred_element_type=F32)
    y_ref[...] = x_ref[...] + _rms(mix, g_ref[...])


def _out_proj(o, x, w_bf16, g, *, tm):
    m, d = x.shape
    return pl.pallas_call(
        _oproj_kernel,
        grid=(m // tm,),
        in_specs=[
            pl.BlockSpec((tm, o.shape[1]), lambda i: (i, 0)),
            pl.BlockSpec((tm, d), lambda i: (i, 0)),
            _resident(w_bf16.shape),
            _resident((1, d)),
        ],
        out_specs=pl.BlockSpec((tm, d), lambda i: (i, 0)),
        out_shape=jax.ShapeDtypeStruct((m, d), F32),
        compiler_params=_params("parallel"),
        name="attn_out_proj",
    )(o, x, w_bf16, g)


def _pool_group(h, prev, pos, w_ref, gi):
    w = POOL_WINDOWS[gi]
    group_dim = h.shape[1] // len(POOL_WINDOWS)
    sl = slice(gi * group_dim, (gi + 1) * group_dim)
    hg = h[:, sl]
    acc = jnp.concatenate([prev[:, sl], hg], axis=0)
    shift = 1
    while shift < w:
        acc = acc + pltpu.roll(acc, shift, 0)
        shift *= 2
    cnt = jnp.minimum(pos + 1, w).astype(F32)
    dgrp = (acc[HALO:, :] / cnt - hg).astype(BF16)
    return jnp.dot(dgrp, w_ref[gi], preferred_element_type=F32)


def _pool_mix(h, prev, pos, w_ref, scale):
    outs = [_pool_group(h, prev, pos, w_ref, gi) for gi in range(len(POOL_WINDOWS))]
    return jnp.concatenate(outs, axis=1) * scale


def _pool_kernel(x_ref, hist_ref, gpre_ref, gpost_ref, w_ref, scale_ref, y_ref, tail_ref, *, pos0):
    x = x_ref[...]
    rows = x.shape[0]
    h = _rms(x, gpre_ref[...])
    tail_ref[...] = h[rows - HALO:, :]
    pos = pos0 + lax.broadcasted_iota(jnp.int32, (rows, 1), 0)
    mix = _pool_mix(h, hist_ref[...], pos, w_ref, scale_ref[...])
    y_ref[...] = x + _rms(mix, gpost_ref[...])


def _pool_mixer(x, hist, gpre, gpost, w_bf16, scale, *, pos0):
    b, t, d = x.shape
    return pl.pallas_call(
        functools.partial(_pool_kernel, pos0=pos0),
        grid=(b,),
        in_specs=[
            pl.BlockSpec((None, t, d), lambda bi: (bi, 0, 0)),
            pl.BlockSpec((None, HALO, d), lambda bi: (bi, 0, 0)),
            _resident((1, d)),
            _resident((1, d)),
            _resident(w_bf16.shape),
            _resident((1, d)),
        ],
        out_specs=[
            pl.BlockSpec((None, t, d), lambda bi: (bi, 0, 0)),
            pl.BlockSpec((None, HALO, d), lambda bi: (bi, 0, 0)),
        ],
        out_shape=[
            jax.ShapeDtypeStruct((b, t, d), F32),
            jax.ShapeDtypeStruct((b, HALO, d), F32),
        ],
        compiler_params=_params("parallel"),
        name="pool_mixer",
    )(x, hist, gpre, gpost, w_bf16, scale)


def _ple_pool_kernel(x_ref, p_ref, wgate_ref, wproj_ref, gpre_ref, gpost_ref, w_ref, scale_ref,
                     y_ref, tail_ref, even_scr, odd_scr, halo_scr, *, tm, tiles_per_seq):
    k = pl.program_id(0)

    @pl.when(k == 0)
    def _():
        odd_scr[...] = jnp.zeros_like(odd_scr)
        halo_scr[...] = jnp.zeros_like(halo_scr)

    def step(dst_scr, src_scr):
        x = x_ref[...]
        xb, pb = x.astype(BF16), p_ref[...].astype(BF16)
        n_groups = len(POOL_WINDOWS)
        n_chunks = 2 * n_groups
        cw = x.shape[1] // n_chunks

        def embed_chunk(c):
            cols = slice(c * cw, (c + 1) * cw)
            gate = jax.nn.sigmoid(jnp.dot(xb, wgate_ref[:, cols], preferred_element_type=F32))
            emb = jnp.dot(pb, wproj_ref[:, cols], preferred_element_type=F32)
            dst_scr[:, cols] = x[:, cols] + gate * emb

        embed_chunk(0)
        seq_tile = lax.rem(jnp.maximum(k - 1, 0), tiles_per_seq)
        x1 = src_scr[...]
        h = _rms(x1, gpre_ref[...])
        prev = jnp.where(seq_tile == 0, 0.0, halo_scr[...])
        pos = seq_tile * tm + lax.broadcasted_iota(jnp.int32, (tm, 1), 0)
        tail = h[tm - HALO:, :]
        halo_scr[...] = tail
        tail_ref[...] = tail
        outs = []
        for gi in range(n_groups):
            outs.append(_pool_group(h, prev, pos, w_ref, gi))
            embed_chunk(gi + 1)
        mix = jnp.concatenate(outs, axis=1) * scale_ref[...]
        y_ref[...] = x1 + _rms(mix, gpost_ref[...])
        for c in range(n_groups + 1, n_chunks):
            embed_chunk(c)

    is_even = lax.rem(k, 2) == 0
    pl.when(is_even)(lambda: step(even_scr, odd_scr))
    pl.when(jnp.logical_not(is_even))(lambda: step(odd_scr, even_scr))


def _ple_pool(x, p, wgate, wproj, gpre, gpost, w_bf16, scale, *, layer, tm):
    b, s, d = x.shape
    nt = s // tm
    n_tiles = b * nt

    def cur(k):
        return jnp.minimum(k, n_tiles - 1)

    def lagged(k):
        return jnp.maximum(k - 1, 0)

    return pl.pallas_call(
        functools.partial(_ple_pool_kernel, tm=tm, tiles_per_seq=nt),
        grid=(n_tiles + 1,),
        in_specs=[
            pl.BlockSpec((None, tm, d), lambda k: (cur(k) // nt, cur(k) % nt, 0)),
            pl.BlockSpec((None, tm, p.shape[2]), lambda k: (layer, cur(k), 0)),
            _resident(wgate.shape),
            _resident(wproj.shape, layer),
            _resident((1, d)),
            _resident((1, d)),
            _resident(w_bf16.shape),
            _resident((1, d)),
        ],
        out_specs=[
            pl.BlockSpec((None, tm, d), lambda k: (lagged(k) // nt, lagged(k) % nt, 0)),
            pl.BlockSpec((None, None, HALO, d), lambda k: (lagged(k) // nt, lagged(k) % nt, 0, 0)),
        ],
        out_shape=[
            jax.ShapeDtypeStruct((b, s, d), F32),
            jax.ShapeDtypeStruct((b, nt, HALO, d), F32),
        ],
        scratch_shapes=[pltpu.VMEM((tm, d), F32), pltpu.VMEM((tm, d), F32), pltpu.VMEM((HALO, d), F32)],
        compiler_params=_params("arbitrary"),
        name="ple_pool",
    )(x, p, wgate, wproj, gpre, gpost, w_bf16, scale)


N_MLP_IN = 5
N_PLE_IN = 3


def _ffn_kernel(*refs, nf, with_ple):
    x_ref, gpre_ref, gpost_ref, wup_ref, wdown_ref = refs[:N_MLP_IN]
    n_in = N_MLP_IN + (N_PLE_IN if with_ple else 0)
    if with_ple:
        p_ref, wgate_ref, wproj_ref = refs[N_MLP_IN:n_in]
    n_cast = (len(refs) - n_in - 2) // 2
    cast_in = refs[n_in:n_in + n_cast]
    y_ref = refs[n_in + n_cast]
    cast_out = refs[n_in + n_cast + 1:n_in + 2 * n_cast + 1]
    h_scr = refs[-1]
    _run_casts(cast_in, cast_out)
    f = pl.program_id(1)

    def mlp_part(h):
        u = jnp.maximum(jnp.dot(h, wup_ref[...], preferred_element_type=F32), 0.0)
        return jnp.dot((u * u).astype(BF16), wdown_ref[...], preferred_element_type=F32)

    @pl.when(f == 0)
    def _():
        h = _rms(x_ref[...], gpre_ref[...]).astype(BF16)
        h_scr[...] = h
        y_ref[...] = mlp_part(h)

    @pl.when(jnp.logical_and(f > 0, f < nf - 1))
    def _():
        y_ref[...] += mlp_part(h_scr[...])

    @pl.when(f == nf - 1)
    def _():
        x1 = x_ref[...] + _rms(y_ref[...] + mlp_part(h_scr[...]), gpost_ref[...])
        if with_ple:
            emb = jnp.dot(p_ref[...].astype(BF16), wproj_ref[...], preferred_element_type=F32)
            gate = jax.nn.sigmoid(jnp.dot(x1.astype(BF16), wgate_ref[...], preferred_element_type=F32))
            x1 = x1 + gate * emb
        y_ref[...] = x1


def _ffn(x, gpre, gpost, wup, wdown, *, tm, tf, ple=None, casts=()):
    m, d = x.shape
    dff = wup.shape[1]
    nf = dff // tf
    assert nf >= 2
    c_in, c_out, c_shapes, c_ops = _cast_specs(casts, (m // tm) * nf, lambda i, f: i * nf + f)
    ple_specs, ple_ops = [], []
    if ple is not None:
        p, wgate, wproj, layer = ple
        ple_specs = [
            pl.BlockSpec((None, tm, p.shape[2]), lambda i, f: (layer, i, 0)),
            _resident(wgate.shape),
            _resident(wproj.shape, layer),
        ]
        ple_ops = [p, wgate, wproj]
    return pl.pallas_call(
        functools.partial(_ffn_kernel, nf=nf, with_ple=ple is not None),
        grid=(m // tm, nf),
        in_specs=[
            pl.BlockSpec((tm, d), lambda i, f: (i, 0)),
            _resident((1, d)),
            _resident((1, d)),
            pl.BlockSpec((d, tf), lambda i, f: (0, f)),
            pl.BlockSpec((tf, d), lambda i, f: (f, 0)),
        ] + ple_specs + c_in,
        out_specs=[pl.BlockSpec((tm, d), lambda i, f: (i, 0))] + c_out,
        out_shape=[jax.ShapeDtypeStruct((m, d), F32)] + c_shapes,
        scratch_shapes=[pltpu.VMEM((tm, d), BF16)],
        compiler_params=_params("parallel", "arbitrary"),
        name="ffn_ple" if ple is not None else "ffn",
    )(x, gpre, gpost, wup, wdown, *ple_ops, *c_ops)


def _row(v):
    return v.reshape(1, -1)


def kernel(x_prompt, x_sample, cache_k, cache_v, state_pool, p_prompt, p_sample, norm_mix_pre, norm_mix_post, norm_ffn_pre, norm_ffn_post, w_qkv, b_qkv, w_o, sinks, w_pool, pool_scale, w_ffn_up, w_ffn_down, w_ple_proj, w_ple_gate):
    b, s, d = x_prompt.shape
    bs, ts, _ = x_sample.shape
    q_dim = w_o.shape[1]
    kv_dim = (w_qkv.shape[2] - q_dim) // 2
    n_kv = kv_dim // HEAD_DIM
    tm = 512
    assert s % tm == 0 and (bs * ts) % tm == 0 and tm % ts == 0

    def bf(w):
        return w.astype(BF16)

    xp = x_prompt.reshape(b * s, d)
    xs = x_sample.reshape(bs * ts, d)
    wproj = bf(w_ple_proj)
    pp = p_prompt.reshape(p_prompt.shape[0], b * s, -1)
    ps = p_sample.reshape(p_sample.shape[0], bs * ts, -1)

    def ffn(x, i, wup, wdown, *, tf, p=None, wgate=None, casts=()):
        ple = None if p is None else (p, wgate, wproj, i)
        return _ffn(x, _row(norm_ffn_pre[i]), _row(norm_ffn_post[i]), wup, wdown, tm=tm, tf=tf, ple=ple, casts=casts)

    tab_p = _rope_tables(jnp.arange(s, dtype=jnp.int32))
    tab_s = tuple(jnp.tile(t, (tm // ts, 1)) for t in _rope_tables(PAST_LEN + jnp.arange(ts, dtype=jnp.int32)))
    wqkv, bqkv = bf(w_qkv[0]), _row(b_qkv[0])
    g_pre, g_post = _row(norm_mix_pre[0]), _row(norm_mix_post[0])

    casts0 = ((w_ffn_up, 0), (w_ffn_down, 0), (w_ple_gate, 0), (w_ple_gate, 1), (w_o, 0))
    q_p, k_p, v_p, *rest = _qkv_rope(xp, g_pre, wqkv, bqkv, tab_p, tm=tm, q_dim=q_dim, kv_dim=kv_dim, casts=casts0)
    kv_p, (wup0, wdown0, wgate0, wgate1, wo) = rest[:4], rest[4:]
    q_s, k_s, v_s, *kv_s = _qkv_rope(xs, g_pre, wqkv, bqkv, tab_s, tm=tm, q_dim=q_dim, kv_dim=kv_dim)

    k_p3, v_p3 = k_p.reshape(b, s, kv_dim), v_p.reshape(b, s, kv_dim)
    k_s3, v_s3 = k_s.reshape(bs, ts, kv_dim), v_s.reshape(bs, ts, kv_dim)
    o_p = _attention(q_p.reshape(b, s, q_dim), *(t.reshape(b, s, -1) for t in kv_p), sinks[0], tq=512)
    cache = (*_pair_layouts(cache_k[0].reshape(bs, WINDOW, kv_dim), n_kv),
             *_pair_layouts(cache_v[0].reshape(bs, WINDOW, kv_dim), n_kv))
    kv_s = (jnp.concatenate([c, t.reshape(bs, ts, -1)], axis=1) for c, t in zip(cache, kv_s))
    o_s = _attention(q_s.reshape(bs, ts, q_dim), *kv_s, sinks[0], tq=ts)

    xp = _out_proj(o_p.reshape(b * s, q_dim), xp, wo, g_post, tm=2 * tm)
    xs = _out_proj(o_s.reshape(bs * ts, q_dim), xs, wo, g_post, tm=tm)

    xp, wup1, wdown1 = ffn(xp, 0, wup0, wdown0, tf=2048, casts=((w_ffn_up, 1), (w_ffn_down, 1)))
    (xs,) = ffn(xs, 0, wup0, wdown0, tf=1024, p=ps, wgate=wgate0)

    g_pre, g_post = _row(norm_mix_pre[1]), _row(norm_mix_post[1])
    wpool, pscale = bf(w_pool[0]), _row(pool_scale[0])
    xp3, tail_p = _ple_pool(xp.reshape(b, s, d), pp, wgate0, wproj, g_pre, g_post, wpool, pscale,
                            layer=0, tm=tm)
    hist = jnp.pad(state_pool[0], ((0, 0), (HALO - POOL_HIST, 0), (0, 0)))
    xs3, tail_s = _pool_mixer(xs.reshape(bs, ts, d), hist, g_pre, g_post, wpool, pscale, pos0=PAST_LEN)

    (xp,) = ffn(xp3.reshape(b * s, d), 1, wup1, wdown1, tf=1024, p=pp, wgate=wgate1)
    (xs,) = ffn(xs3.reshape(bs * ts, d), 1, wup1, wdown1, tf=1024, p=ps, wgate=wgate1)

    n_heads_kv = (n_kv, HEAD_DIM)
    return (
        xp.reshape(b, s, d),
        xs.reshape(bs, ts, d),
        k_p3[:, s - WINDOW:].reshape(1, b, WINDOW, *n_heads_kv),
        v_p3[:, s - WINDOW:].reshape(1, b, WINDOW, *n_heads_kv),
        tail_p[:, -1, HALO - POOL_HIST:][None],
        k_s3.reshape(1, bs, ts, *n_heads_kv),
        v_s3.reshape(1, bs, ts, *n_heads_kv),
        tail_s[:, HALO - POOL_HIST:][None],
    )
```

```python
import functools
import math

import jax
import jax.numpy as jnp
from jax import lax
from jax.experimental import pallas as pl
from jax.experimental.pallas import tpu as pltpu

HEAD_DIM = 64
GQA_GROUP = 8
CHUNK = 64
WINDOW = 128
ROT_DIM = HEAD_DIM // 4
ROPE_THETA = 500000.0
POOL_WINDOWS = (2, 4, 8, 16)
POOL_HIST = max(POOL_WINDOWS) - 1
PAST_LEN = 2048
EPS = 1e-6

LANES = 128
BF16_ROWS = 16
HALO = 16
SUB_ROWS = 256
VMEM_LIMIT_BYTES = 60 * 1024 * 1024

F32 = jnp.float32
BF16 = jnp.bfloat16


def _rms(x, g):
    ms = jnp.mean(x * x, axis=-1, keepdims=True)
    return x * lax.rsqrt(ms + EPS) * g


def _resident(shape, layer=None):
    if layer is None:
        nd = len(shape)
        return pl.BlockSpec(shape, lambda *_: (0,) * nd, pipeline_mode=pl.Buffered(1))
    nd = len(shape) - 1
    return pl.BlockSpec((None,) + tuple(shape[1:]), lambda *_: (layer,) + (0,) * nd, pipeline_mode=pl.Buffered(1))


def _params(*sem):
    return pltpu.CompilerParams(dimension_semantics=sem, vmem_limit_bytes=VMEM_LIMIT_BYTES)


def _cast_specs(casts, n_steps, flat_step):
    in_specs, out_specs, out_shapes, operands = [], [], [], []
    for w, layer in casts:
        _, r, c = w.shape
        rb = max(BF16_ROWS, r // n_steps)
        csplit = rb * n_steps // r
        assert r % rb == 0 and (r // rb) * csplit == n_steps and c % (csplit * LANES) == 0
        cw = c // csplit

        def in_map(*idx, layer=layer, csplit=csplit):
            k = flat_step(*idx)
            return (layer, k // csplit, k % csplit)

        def out_map(*idx, csplit=csplit):
            k = flat_step(*idx)
            return (k // csplit, k % csplit)

        in_specs.append(pl.BlockSpec((None, rb, cw), in_map))
        out_specs.append(pl.BlockSpec((rb, cw), out_map))
        out_shapes.append(jax.ShapeDtypeStruct((r, c), BF16))
        operands.append(w)
    return in_specs, out_specs, out_shapes, operands


def _run_casts(srcs, dsts):
    for src, dst in zip(srcs, dsts, strict=True):
        dst[...] = src[...].astype(BF16)


def _rope_tables(positions):
    half = ROT_DIM // 2
    inv = ROPE_THETA ** (-jnp.arange(0, ROT_DIM, 2, dtype=F32) / ROT_DIM)
    ang = positions.astype(F32)[:, None] * inv[None, :]
    cos, sin = jnp.cos(ang), jnp.sin(ang)
    p = positions.shape[0]
    ones = jnp.ones((p, HEAD_DIM - ROT_DIM), F32)
    zeros = jnp.zeros((p, HEAD_DIM - ROT_DIM), F32)
    zh = jnp.zeros((p, half), F32)
    c = jnp.concatenate([cos, cos, ones], axis=1)
    sa = jnp.concatenate([-sin, zh, zeros], axis=1)
    sb = jnp.concatenate([zh, sin, zeros], axis=1)
    rep = LANES // HEAD_DIM
    return tuple(jnp.tile(t, (1, rep)) for t in (c, sa, sb))


def _store_pair_layouts(blk, a_ref, b_ref, rows, col):
    lo = lax.broadcasted_iota(jnp.int32, blk.shape, 1) < HEAD_DIM
    swapped = pltpu.roll(blk, HEAD_DIM, 1)
    zero = jnp.zeros_like(blk)
    a_ref[rows, col:col + LANES] = jnp.where(lo, blk, zero).astype(BF16)
    b_ref[rows, col:col + LANES] = jnp.where(lo, zero, swapped).astype(BF16)
    a_ref[rows, col + LANES:col + 2 * LANES] = jnp.where(lo, swapped, zero).astype(BF16)
    b_ref[rows, col + LANES:col + 2 * LANES] = jnp.where(lo, zero, blk).astype(BF16)


N_QKV_IN = 7
N_QKV_OUT = 7


def _qkv_kernel(*refs, q_dim, kv_dim, nchunk):
    x_ref, g_ref, w_ref, b_ref, c_ref, sa_ref, sb_ref = refs[:N_QKV_IN]
    n_cast = (len(refs) - N_QKV_IN - N_QKV_OUT) // 2
    cast_in = refs[N_QKV_IN:N_QKV_IN + n_cast]
    q_ref, k_ref, v_ref, ka_ref, kb_ref, va_ref, vb_ref = refs[N_QKV_IN + n_cast:N_QKV_IN + n_cast + N_QKV_OUT]
    cast_out = refs[N_QKV_IN + n_cast + N_QKV_OUT:]
    _run_casts(cast_in, cast_out)

    scale = 1.0 / math.sqrt(HEAD_DIM)
    half = ROT_DIM // 2
    heads_per_blk = LANES // HEAD_DIM
    rot_cols = q_dim + kv_dim
    total = q_dim + 2 * kv_dim
    for r0 in range(0, x_ref.shape[0], SUB_ROWS):
        rows = slice(r0, r0 + SUB_ROWS)
        h = _rms(x_ref[rows, :], g_ref[...]).astype(BF16)
        c, sa, sb = c_ref[rows, :], sa_ref[rows, :], sb_ref[rows, :]
        cq, saq, sbq = c * scale, sa * scale, sb * scale
        for n0 in range(0, total, nchunk):
            acc = jnp.dot(h, w_ref[:, n0:n0 + nchunk], preferred_element_type=F32) + b_ref[:, n0:n0 + nchunk]
            for j0 in range(0, nchunk, LANES):
                col = n0 + j0
                blk = acc[:, j0:j0 + LANES]
                if col < q_dim:
                    blk = blk * cq + pltpu.roll(blk, LANES - half, 1) * saq + pltpu.roll(blk, half, 1) * sbq
                    q_ref[rows, col:col + LANES] = blk.astype(BF16)
                elif col < rot_cols:
                    blk = blk * c + pltpu.roll(blk, LANES - half, 1) * sa + pltpu.roll(blk, half, 1) * sb
                    k_ref[rows, col - q_dim:col - q_dim + LANES] = blk
                    _store_pair_layouts(blk, ka_ref, kb_ref, rows, (col - q_dim) * heads_per_blk)
                else:
                    v_ref[rows, col - rot_cols:col - rot_cols + LANES] = blk
                    _store_pair_layouts(blk, va_ref, vb_ref, rows, (col - rot_cols) * heads_per_blk)


def _qkv_rope(x, g, w_bf16, b, tables, *, tm, q_dim, kv_dim, casts=()):
    m, d = x.shape
    total = q_dim + 2 * kv_dim
    ptiles = tables[0].shape[0] // tm
    tab_spec = pl.BlockSpec((tm, LANES), lambda i: (i % ptiles, 0))
    kern = functools.partial(_qkv_kernel, q_dim=q_dim, kv_dim=kv_dim, nchunk=256)
    pair_w = kv_dim * (LANES // HEAD_DIM)
    c_in, c_out, c_shapes, c_ops = _cast_specs(casts, m // tm, lambda i: i)

    def rows(width):
        return pl.BlockSpec((tm, width), lambda i: (i, 0))

    return pl.pallas_call(
        kern,
        grid=(m // tm,),
        in_specs=[
            rows(d),
            _resident((1, d)),
            _resident((d, total)),
            _resident((1, total)),
            tab_spec, tab_spec, tab_spec,
        ] + c_in,
        out_specs=[rows(q_dim), rows(kv_dim), rows(kv_dim), rows(pair_w), rows(pair_w), rows(pair_w), rows(pair_w)]
        + c_out,
        out_shape=[
            jax.ShapeDtypeStruct((m, q_dim), BF16),
            jax.ShapeDtypeStruct((m, kv_dim), F32),
            jax.ShapeDtypeStruct((m, kv_dim), F32),
        ] + [jax.ShapeDtypeStruct((m, pair_w), BF16)] * 4 + c_shapes,
        compiler_params=_params("parallel"),
        name="qkv_rope",
    )(x, g, w_bf16, b, *tables, *c_ops)


def _chunks_per_body(chunks):
    return next(g for g in (2, 1) if chunks % g == 0)


def _attn_kernel(sink_ref, q_ref, ka_ref, kb_ref, va_ref, vb_ref, o_ref, s_scr, *, n_kv, chunks, group, hist_chunks):
    step = pl.program_id(1)
    span = WINDOW + CHUNK
    pairs = GQA_GROUP // 2
    win_chunks = WINDOW // CHUNK
    neg_inf = -jnp.inf
    row_k = lax.broadcasted_iota(jnp.int32, (2 * span, LANES), 0)
    lane_k = lax.broadcasted_iota(jnp.int32, (2 * span, LANES), 1)
    ones_ab = jnp.where((row_k < span) == (lane_k < HEAD_DIM), 1.0, 0.0).astype(BF16)
    lo = lax.broadcasted_iota(jnp.int32, (CHUNK, LANES), 1) < HEAD_DIM
    nt = (((1,), (1,)), ((), ()))

    def body(j, carry, *, masked):
        units = []
        for u in range(group):
            ci = j * group + u
            ckv = step * chunks + ci + hist_chunks
            k0 = pl.multiple_of(jnp.maximum(ckv - win_chunks, 0) * CHUNK, CHUNK)
            units.append((pl.ds(pl.multiple_of(ci * CHUNK, CHUNK), CHUNK), pl.ds(k0, span), ckv))

        for u, (rows, win, _) in enumerate(units):
            for kh in range(n_kv):
                kcols = slice(kh * LANES, (kh + 1) * LANES)
                q_pairs = jnp.concatenate(
                    [q_ref[rows, (kh * pairs + p) * LANES:(kh * pairs + p + 1) * LANES] for p in range(pairs)], axis=0)
                k_ab = jnp.concatenate([ka_ref[win, kcols], kb_ref[win, kcols]], axis=0)
                s_scr[u, kh] = lax.dot_general(q_pairs, k_ab, nt, preferred_element_type=F32)

        e_all, t_all = [], []
        for u, (_, _, ckv) in enumerate(units):
            if masked:
                key_row = lax.broadcasted_iota(jnp.int32, (1, span), 1)
                bias1 = jnp.where(key_row < (ckv + 1) * CHUNK, 0.0, neg_inf)
                bias = jnp.concatenate([bias1, bias1], axis=1)
            for kh in range(n_kv):
                e_rows, t_rows = [], []
                for p in range(pairs):
                    s = s_scr[u, kh, p * CHUNK:(p + 1) * CHUNK, :]
                    if masked:
                        s = s + bias
                    t0, t1, t2 = s[:, :LANES], s[:, LANES:2 * LANES], s[:, 2 * LANES:]
                    sk_a = sink_ref[kh * GQA_GROUP + 2 * p]
                    sk_b = sink_ref[kh * GQA_GROUP + 2 * p + 1]
                    m_a = jnp.max(jnp.maximum(t0, jnp.where(lo, t1, neg_inf)), axis=-1, keepdims=True)
                    m_b = jnp.max(jnp.maximum(t2, jnp.where(lo, neg_inf, t1)), axis=-1, keepdims=True)
                    m_a, m_b = jnp.maximum(m_a, sk_a), jnp.maximum(m_b, sk_b)
                    m_ab = jnp.where(lo, m_a, m_b)
                    e = jnp.concatenate([jnp.exp(t0 - m_a), jnp.exp(t1 - m_ab), jnp.exp(t2 - m_b)], axis=1)
                    e_rows.append(e.astype(BF16))
                    t_rows.append(jnp.exp(jnp.where(lo[:1], sk_a, sk_b) - m_ab))
                e_all.append(jnp.concatenate(e_rows, axis=0))
                t_all.append(jnp.concatenate(t_rows, axis=0))

        for u, (rows, win, _) in enumerate(units):
            for kh in range(n_kv):
                kcols = slice(kh * LANES, (kh + 1) * LANES)
                v_ab = jnp.concatenate([va_ref[win, kcols], vb_ref[win, kcols]], axis=0)
                rhs = jnp.concatenate([v_ab, ones_ab], axis=1)
                acc = jnp.dot(e_all[u * n_kv + kh], rhs, preferred_element_type=F32)
                o_pairs = (acc[:, :LANES] / (acc[:, LANES:] + t_all[u * n_kv + kh])).astype(BF16)
                for p in range(pairs):
                    col = (kh * pairs + p) * LANES
                    o_ref[rows, col:col + LANES] = o_pairs[p * CHUNK:(p + 1) * CHUNK, :]
        return carry

    def run(masked):
        lax.fori_loop(0, chunks // group, functools.partial(body, masked=masked), 0)

    if hist_chunks >= win_chunks:
        run(False)
    else:
        needs_mask = step * chunks + hist_chunks < win_chunks
        pl.when(needs_mask)(lambda: run(True))
        pl.when(jnp.logical_not(needs_mask))(lambda: run(False))


def _attention(q, ka, kb, va, vb, sinks, *, tq):
    b, s, q_dim = q.shape
    rows, kvw = ka.shape[1:]
    n_kv = kvw // LANES
    span = WINDOW + CHUNK
    chunks = tq // CHUNK
    assert rows >= span and (rows - s) % CHUNK == 0
    hist_spec = pl.BlockSpec((None, rows, kvw), lambda i, c: (i, 0, 0))
    group = _chunks_per_body(chunks)
    kern = functools.partial(_attn_kernel, n_kv=n_kv, chunks=chunks, group=group, hist_chunks=(rows - s) // CHUNK)
    return pl.pallas_call(
        kern,
        grid=(b, s // tq),
        in_specs=[
            pl.BlockSpec(memory_space=pltpu.SMEM),
            pl.BlockSpec((None, tq, q_dim), lambda i, c: (i, c, 0)),
            hist_spec, hist_spec, hist_spec, hist_spec,
        ],
        out_specs=pl.BlockSpec((None, tq, q_dim), lambda i, c: (i, c, 0)),
        out_shape=jax.ShapeDtypeStruct((b, s, q_dim), BF16),
        scratch_shapes=[pltpu.VMEM((group, n_kv, (GQA_GROUP // 2) * CHUNK, 2 * span), F32)],
        compiler_params=_params("parallel", "arbitrary"),
        name="swa_attention",
    )(sinks, q, ka, kb, va, vb)


def _pair_layouts(x3, n_kv):
    b, r, _ = x3.shape
    x4 = x3.astype(BF16).reshape(b, r, n_kv, HEAD_DIM)
    lo = jnp.pad(x4, ((0, 0), (0, 0), (0, 0), (0, LANES - HEAD_DIM)))
    hi = jnp.pad(x4, ((0, 0), (0, 0), (0, 0), (LANES - HEAD_DIM, 0)))
    return lo.reshape(b, r, n_kv * LANES), hi.reshape(b, r, n_kv * LANES)


def _oproj_kernel(o_ref, x_ref, w_ref, g_ref, y_ref):
    g = g_ref[...]
    for r0 in range(0, o_ref.shape[0], SUB_ROWS):
        rows = slice(r0, r0 + SUB_ROWS)
        mix = jnp.dot(o_ref[rows, :], w_ref[...], preferred_element_type=F32)
        y_ref[rows, :] = x_ref[rows, :] + _rms(mix, g)


def _out_proj(o, x, w_bf16, g, *, tm):
    m, d = x.shape
    return pl.pallas_call(
        _oproj_kernel,
        grid=(m // tm,),
        in_specs=[
            pl.BlockSpec((tm, o.shape[1]), lambda i: (i, 0)),
            pl.BlockSpec((tm, d), lambda i: (i, 0)),
            _resident(w_bf16.shape),
            _resident((1, d)),
        ],
        out_specs=pl.BlockSpec((tm, d), lambda i: (i, 0)),
        out_shape=jax.ShapeDtypeStruct((m, d), F32),
        compiler_params=_params("parallel"),
        name="attn_out_proj",
    )(o, x, w_bf16, g)


def _pool_group(h, prev, pos, w_ref, gi):
    w = POOL_WINDOWS[gi]
    group_dim = h.shape[1] // len(POOL_WINDOWS)
    sl = slice(gi * group_dim, (gi + 1) * group_dim)
    hg = h[:, sl]
    acc = jnp.concatenate([prev[:, sl], hg], axis=0)
    shift = 1
    while shift < w:
        acc = acc + pltpu.roll(acc, shift, 0)
        shift *= 2
    cnt = jnp.minimum(pos + 1, w).astype(F32)
    dgrp = (acc[HALO:, :] / cnt - hg).astype(BF16)
    return jnp.dot(dgrp, w_ref[gi], preferred_element_type=F32)


def _pool_mix(h, prev, pos, w_ref, scale):
    outs = [_pool_group(h, prev, pos, w_ref, gi) for gi in range(len(POOL_WINDOWS))]
    return jnp.concatenate(outs, axis=1) * scale


def _pool_kernel(x_ref, hist_ref, gpre_ref, gpost_ref, w_ref, scale_ref, y_ref, tail_ref, *, pos0):
    x = x_ref[...]
    rows = x.shape[0]
    h = _rms(x, gpre_ref[...])
    tail_ref[...] = h[rows - HALO:, :]
    pos = pos0 + lax.broadcasted_iota(jnp.int32, (rows, 1), 0)
    mix = _pool_mix(h, hist_ref[...], pos, w_ref, scale_ref[...])
    y_ref[...] = x + _rms(mix, gpost_ref[...])


def _pool_mixer(x, hist, gpre, gpost, w_bf16, scale, *, pos0):
    b, t, d = x.shape
    return pl.pallas_call(
        functools.partial(_pool_kernel, pos0=pos0),
        grid=(b,),
        in_specs=[
            pl.BlockSpec((None, t, d), lambda bi: (bi, 0, 0)),
            pl.BlockSpec((None, HALO, d), lambda bi: (bi, 0, 0)),
            _resident((1, d)),
            _resident((1, d)),
            _resident(w_bf16.shape),
            _resident((1, d)),
        ],
        out_specs=[
            pl.BlockSpec((None, t, d), lambda bi: (bi, 0, 0)),
            pl.BlockSpec((None, HALO, d), lambda bi: (bi, 0, 0)),
        ],
        out_shape=[
            jax.ShapeDtypeStruct((b, t, d), F32),
            jax.ShapeDtypeStruct((b, HALO, d), F32),
        ],
        compiler_params=_params("parallel"),
        name="pool_mixer",
    )(x, hist, gpre, gpost, w_bf16, scale)


def _ple_pool_kernel(x_ref, p_ref, wgate_ref, wproj_ref, gpre_ref, gpost_ref, w_ref, scale_ref,
                     y_ref, tail_ref, even_scr, odd_scr, halo_scr, *, tm, tiles_per_seq):
    k = pl.program_id(0)

    @pl.when(k == 0)
    def _():
        odd_scr[...] = jnp.zeros_like(odd_scr)
        halo_scr[...] = jnp.zeros_like(halo_scr)

    def step(dst_scr, src_scr):
        x = x_ref[...]
        xb, pb = x.astype(BF16), p_ref[...].astype(BF16)
        n_groups = len(POOL_WINDOWS)
        n_chunks = 2 * n_groups
        cw = x.shape[1] // n_chunks

        def embed_chunk(c):
            cols = slice(c * cw, (c + 1) * cw)
            gate = jax.nn.sigmoid(jnp.dot(xb, wgate_ref[:, cols], preferred_element_type=F32))
            emb = jnp.dot(pb, wproj_ref[:, cols], preferred_element_type=F32)
            dst_scr[:, cols] = x[:, cols] + gate * emb

        embed_chunk(0)
        seq_tile = lax.rem(jnp.maximum(k - 1, 0), tiles_per_seq)
        x1 = src_scr[...]
        h = _rms(x1, gpre_ref[...])
        prev = jnp.where(seq_tile == 0, 0.0, halo_scr[...])
        pos = seq_tile * tm + lax.broadcasted_iota(jnp.int32, (tm, 1), 0)
        tail = h[tm - HALO:, :]
        halo_scr[...] = tail
        tail_ref[...] = tail
        outs = []
        for gi in range(n_groups):
            outs.append(_pool_group(h, prev, pos, w_ref, gi))
            embed_chunk(gi + 1)
        mix = jnp.concatenate(outs, axis=1) * scale_ref[...]
        y_ref[...] = x1 + _rms(mix, gpost_ref[...])
        for c in range(n_groups + 1, n_chunks):
            embed_chunk(c)

    is_even = lax.rem(k, 2) == 0
    pl.when(is_even)(lambda: step(even_scr, odd_scr))
    pl.when(jnp.logical_not(is_even))(lambda: step(odd_scr, even_scr))


def _ple_pool(x, p, wgate, wproj, gpre, gpost, w_bf16, scale, *, layer, tm):
    b, s, d = x.shape
    nt = s // tm
    n_tiles = b * nt

    def cur(k):
        return jnp.minimum(k, n_tiles - 1)

    def lagged(k):
        return jnp.maximum(k - 1, 0)

    return pl.pallas_call(
        functools.partial(_ple_pool_kernel, tm=tm, tiles_per_seq=nt),
        grid=(n_tiles + 1,),
        in_specs=[
            pl.BlockSpec((None, tm, d), lambda k: (cur(k) // nt, cur(k) % nt, 0)),
            pl.BlockSpec((None, tm, p.shape[2]), lambda k: (layer, cur(k), 0)),
            _resident(wgate.shape),
            _resident(wproj.shape, layer),
            _resident((1, d)),
            _resident((1, d)),
            _resident(w_bf16.shape),
            _resident((1, d)),
        ],
        out_specs=[
            pl.BlockSpec((None, tm, d), lambda k: (lagged(k) // nt, lagged(k) % nt, 0)),
            pl.BlockSpec((None, None, HALO, d), lambda k: (lagged(k) // nt, lagged(k) % nt, 0, 0)),
        ],
        out_shape=[
            jax.ShapeDtypeStruct((b, s, d), F32),
            jax.ShapeDtypeStruct((b, nt, HALO, d), F32),
        ],
        scratch_shapes=[pltpu.VMEM((tm, d), F32), pltpu.VMEM((tm, d), F32), pltpu.VMEM((HALO, d), F32)],
        compiler_params=_params("arbitrary"),
        name="ple_pool",
    )(x, p, wgate, wproj, gpre, gpost, w_bf16, scale)


def _ple_kernel(x_ref, p_ref, wgate_ref, wproj_ref, y_ref):
    for r0 in range(0, x_ref.shape[0], SUB_ROWS):
        rows = slice(r0, r0 + SUB_ROWS)
        x = x_ref[rows, :]
        gate = jax.nn.sigmoid(jnp.dot(x.astype(BF16), wgate_ref[...], preferred_element_type=F32))
        emb = jnp.dot(p_ref[rows, :].astype(BF16), wproj_ref[...], preferred_element_type=F32)
        y_ref[rows, :] = x + gate * emb


def _ple(x, p, wgate, wproj, *, layer, tm):
    m, d = x.shape
    return pl.pallas_call(
        _ple_kernel,
        grid=(m // tm,),
        in_specs=[
            pl.BlockSpec((tm, d), lambda i: (i, 0)),
            pl.BlockSpec((None, tm, p.shape[2]), lambda i: (layer, i, 0)),
            _resident(wgate.shape),
            _resident(wproj.shape, layer),
        ],
        out_specs=pl.BlockSpec((tm, d), lambda i: (i, 0)),
        out_shape=jax.ShapeDtypeStruct((m, d), F32),
        compiler_params=_params("parallel"),
        name="ple",
    )(x, p, wgate, wproj)


N_MLP_IN = 5
N_PLE_IN = 3


def _ffn_kernel(*refs, nf, with_ple):
    x_ref, gpre_ref, gpost_ref, wup_ref, wdown_ref = refs[:N_MLP_IN]
    n_in = N_MLP_IN + (N_PLE_IN if with_ple else 0)
    if with_ple:
        p_ref, wgate_ref, wproj_ref = refs[N_MLP_IN:n_in]
    n_cast = (len(refs) - n_in - 2) // 2
    cast_in = refs[n_in:n_in + n_cast]
    y_ref = refs[n_in + n_cast]
    cast_out = refs[n_in + n_cast + 1:n_in + 2 * n_cast + 1]
    h_scr = refs[-1]
    _run_casts(cast_in, cast_out)
    f = pl.program_id(1)

    def mlp_part(h):
        u = jnp.maximum(jnp.dot(h, wup_ref[...], preferred_element_type=F32), 0.0)
        return jnp.dot((u * u).astype(BF16), wdown_ref[...], preferred_element_type=F32)

    @pl.when(f == 0)
    def _():
        h = _rms(x_ref[...], gpre_ref[...]).astype(BF16)
        h_scr[...] = h
        y_ref[...] = mlp_part(h)

    @pl.when(jnp.logical_and(f > 0, f < nf - 1))
    def _():
        y_ref[...] += mlp_part(h_scr[...])

    @pl.when(f == nf - 1)
    def _():
        x1 = x_ref[...] + _rms(y_ref[...] + mlp_part(h_scr[...]), gpost_ref[...])
        if with_ple:
            emb = jnp.dot(p_ref[...].astype(BF16), wproj_ref[...], preferred_element_type=F32)
            gate = jax.nn.sigmoid(jnp.dot(x1.astype(BF16), wgate_ref[...], preferred_element_type=F32))
            x1 = x1 + gate * emb
        y_ref[...] = x1


def _ffn(x, gpre, gpost, wup, wdown, *, tm, tf, ple=None, casts=()):
    m, d = x.shape
    dff = wup.shape[1]
    nf = dff // tf
    assert nf >= 2
    c_in, c_out, c_shapes, c_ops = _cast_specs(casts, (m // tm) * nf, lambda i, f: i * nf + f)
    ple_specs, ple_ops = [], []
    if ple is not None:
        p, wgate, wproj, layer = ple
        ple_specs = [
            pl.BlockSpec((None, tm, p.shape[2]), lambda i, f: (layer, i, 0)),
            _resident(wgate.shape),
            _resident(wproj.shape, layer),
        ]
        ple_ops = [p, wgate, wproj]
    return pl.pallas_call(
        functools.partial(_ffn_kernel, nf=nf, with_ple=ple is not None),
        grid=(m // tm, nf),
        in_specs=[
            pl.BlockSpec((tm, d), lambda i, f: (i, 0)),
            _resident((1, d)),
            _resident((1, d)),
            pl.BlockSpec((d, tf), lambda i, f: (0, f)),
            pl.BlockSpec((tf, d), lambda i, f: (f, 0)),
        ] + ple_specs + c_in,
        out_specs=[pl.BlockSpec((tm, d), lambda i, f: (i, 0))] + c_out,
        out_shape=[jax.ShapeDtypeStruct((m, d), F32)] + c_shapes,
        scratch_shapes=[pltpu.VMEM((tm, d), BF16)],
        compiler_params=_params("parallel", "arbitrary"),
        name="ffn_ple" if ple is not None else "ffn",
    )(x, gpre, gpost, wup, wdown, *ple_ops, *c_ops)


def _row(v):
    return v.reshape(1, -1)


def kernel(x_prompt, x_sample, cache_k, cache_v, state_pool, p_prompt, p_sample, norm_mix_pre, norm_mix_post, norm_ffn_pre, norm_ffn_post, w_qkv, b_qkv, w_o, sinks, w_pool, pool_scale, w_ffn_up, w_ffn_down, w_ple_proj, w_ple_gate):
    b, s, d = x_prompt.shape
    bs, ts, _ = x_sample.shape
    q_dim = w_o.shape[1]
    kv_dim = (w_qkv.shape[2] - q_dim) // 2
    n_kv = kv_dim // HEAD_DIM
    tm = 512
    assert s % tm == 0 and (bs * ts) % tm == 0 and tm % ts == 0

    def bf(w):
        return w.astype(BF16)

    xp = x_prompt.reshape(b * s, d)
    xs = x_sample.reshape(bs * ts, d)
    wproj = bf(w_ple_proj)
    pp = p_prompt.reshape(p_prompt.shape[0], b * s, -1)
    ps = p_sample.reshape(p_sample.shape[0], bs * ts, -1)

    def ffn(x, i, wup, wdown, *, tf, p=None, wgate=None, casts=()):
        ple = None if p is None else (p, wgate, wproj, i)
        return _ffn(x, _row(norm_ffn_pre[i]), _row(norm_ffn_post[i]), wup, wdown, tm=tm, tf=tf, ple=ple, casts=casts)

    tab_p = _rope_tables(jnp.arange(s, dtype=jnp.int32))
    tab_s = tuple(jnp.tile(t, (tm // ts, 1)) for t in _rope_tables(PAST_LEN + jnp.arange(ts, dtype=jnp.int32)))
    wqkv, bqkv = bf(w_qkv[0]), _row(b_qkv[0])
    g_pre, g_post = _row(norm_mix_pre[0]), _row(norm_mix_post[0])

    casts0 = ((w_ffn_up, 0), (w_ffn_down, 0), (w_ple_gate, 0), (w_ple_gate, 1), (w_o, 0))
    q_p, k_p, v_p, *rest = _qkv_rope(xp, g_pre, wqkv, bqkv, tab_p, tm=tm, q_dim=q_dim, kv_dim=kv_dim, casts=casts0)
    kv_p, (wup0, wdown0, wgate0, wgate1, wo) = rest[:4], rest[4:]
    q_s, k_s, v_s, *kv_s = _qkv_rope(xs, g_pre, wqkv, bqkv, tab_s, tm=tm, q_dim=q_dim, kv_dim=kv_dim)

    k_p3, v_p3 = k_p.reshape(b, s, kv_dim), v_p.reshape(b, s, kv_dim)
    k_s3, v_s3 = k_s.reshape(bs, ts, kv_dim), v_s.reshape(bs, ts, kv_dim)
    o_p = _attention(q_p.reshape(b, s, q_dim), *(t.reshape(b, s, -1) for t in kv_p), sinks[0], tq=512)
    cache = (*_pair_layouts(cache_k[0].reshape(bs, WINDOW, kv_dim), n_kv),
             *_pair_layouts(cache_v[0].reshape(bs, WINDOW, kv_dim), n_kv))
    kv_s = (jnp.concatenate([c, t.reshape(bs, ts, -1)], axis=1) for c, t in zip(cache, kv_s))
    o_s = _attention(q_s.reshape(bs, ts, q_dim), *kv_s, sinks[0], tq=ts)

    xp = _out_proj(o_p.reshape(b * s, q_dim), xp, wo, g_post, tm=2 * tm)
    xs = _out_proj(o_s.reshape(bs * ts, q_dim), xs, wo, g_post, tm=tm)

    xp, wup1, wdown1 = ffn(xp, 0, wup0, wdown0, tf=2048, casts=((w_ffn_up, 1), (w_ffn_down, 1)))
    (xs,) = ffn(xs, 0, wup0, wdown0, tf=1024, p=ps, wgate=wgate0)

    g_pre, g_post = _row(norm_mix_pre[1]), _row(norm_mix_post[1])
    wpool, pscale = bf(w_pool[0]), _row(pool_scale[0])
    xp3, tail_p = _ple_pool(xp.reshape(b, s, d), pp, wgate0, wproj, g_pre, g_post, wpool, pscale,
                            layer=0, tm=tm)
    hist = jnp.pad(state_pool[0], ((0, 0), (HALO - POOL_HIST, 0), (0, 0)))
    xs3, tail_s = _pool_mixer(xs.reshape(bs, ts, d), hist, g_pre, g_post, wpool, pscale, pos0=PAST_LEN)

    (xp,) = ffn(xp3.reshape(b * s, d), 1, wup1, wdown1, tf=2048)
    xp = _ple(xp, pp, wgate1, wproj, layer=1, tm=2 * tm)
    (xs,) = ffn(xs3.reshape(bs * ts, d), 1, wup1, wdown1, tf=1024, p=ps, wgate=wgate1)

    n_heads_kv = (n_kv, HEAD_DIM)
    return (
        xp.reshape(b, s, d),
        xs.reshape(bs, ts, d),
        k_p3[:, s - WINDOW:].reshape(1, b, WINDOW, *n_heads_kv),
        v_p3[:, s - WINDOW:].reshape(1, b, WINDOW, *n_heads_kv),
        tail_p[:, -1, HALO - POOL_HIST:][None],
        k_s3.reshape(1, bs, ts, *n_heads_kv),
        v_s3.reshape(1, bs, ts, *n_heads_kv),
        tail_s[:, HALO - POOL_HIST:][None],
    )
```

```python
import functools
import math
from typing import NamedTuple

import jax
import jax.numpy as jnp
from jax import lax
from jax.experimental import pallas as pl
from jax.experimental.pallas import tpu as pltpu

HEAD_DIM = 64
GQA_GROUP = 8
CHUNK = 64
WINDOW = 128
ROT_DIM = HEAD_DIM // 4
ROPE_THETA = 500000.0
POOL_WINDOWS = (2, 4, 8, 16)
POOL_HIST = max(POOL_WINDOWS) - 1
PAST_LEN = 2048
EPS = 1e-6

LANES = 128
BF16_ROWS = 16
HALO = 16
SUB_ROWS = 256
VMEM_LIMIT_BYTES = 60 * 1024 * 1024


class _Tiles(NamedTuple):
    rows: int = 512
    rows_wide: int = 1024
    attn_rows: int = 512
    dff: int = 2048
    dff_ple: int = 1024


TILES = _Tiles()

F32 = jnp.float32
BF16 = jnp.bfloat16


def _rms(x, g):
    ms = jnp.mean(x * x, axis=-1, keepdims=True)
    return x * lax.rsqrt(ms + EPS) * g


def _resident(shape, layer=None):
    if layer is None:
        nd = len(shape)
        return pl.BlockSpec(shape, lambda *_: (0,) * nd, pipeline_mode=pl.Buffered(1))
    nd = len(shape) - 1
    return pl.BlockSpec((None,) + tuple(shape[1:]), lambda *_: (layer,) + (0,) * nd, pipeline_mode=pl.Buffered(1))


def _params(*sem):
    return pltpu.CompilerParams(dimension_semantics=sem, vmem_limit_bytes=VMEM_LIMIT_BYTES)


def _cast_specs(casts, n_steps, flat_step):
    in_specs, out_specs, out_shapes, operands = [], [], [], []
    for w, layer in casts:
        _, r, c = w.shape
        rb = max(BF16_ROWS, r // n_steps)
        csplit = rb * n_steps // r
        assert r % rb == 0 and (r // rb) * csplit == n_steps and c % (csplit * LANES) == 0
        cw = c // csplit

        def in_map(*idx, layer=layer, csplit=csplit):
            k = flat_step(*idx)
            return (layer, k // csplit, k % csplit)

        def out_map(*idx, csplit=csplit):
            k = flat_step(*idx)
            return (k // csplit, k % csplit)

        in_specs.append(pl.BlockSpec((None, rb, cw), in_map))
        out_specs.append(pl.BlockSpec((rb, cw), out_map))
        out_shapes.append(jax.ShapeDtypeStruct((r, c), BF16))
        operands.append(w)
    return in_specs, out_specs, out_shapes, operands


def _run_casts(srcs, dsts):
    for src, dst in zip(srcs, dsts, strict=True):
        dst[...] = src[...].astype(BF16)


def _rope_tables(positions):
    half = ROT_DIM // 2
    inv = ROPE_THETA ** (-jnp.arange(0, ROT_DIM, 2, dtype=F32) / ROT_DIM)
    ang = positions.astype(F32)[:, None] * inv[None, :]
    cos, sin = jnp.cos(ang), jnp.sin(ang)
    p = positions.shape[0]
    ones = jnp.ones((p, HEAD_DIM - ROT_DIM), F32)
    zeros = jnp.zeros((p, HEAD_DIM - ROT_DIM), F32)
    zh = jnp.zeros((p, half), F32)
    c = jnp.concatenate([cos, cos, ones], axis=1)
    sa = jnp.concatenate([-sin, zh, zeros], axis=1)
    sb = jnp.concatenate([zh, sin, zeros], axis=1)
    rep = LANES // HEAD_DIM
    return tuple(jnp.tile(t, (1, rep)) for t in (c, sa, sb))


def _store_pair_layouts(blk, a_ref, b_ref, rows, col):
    lo = lax.broadcasted_iota(jnp.int32, blk.shape, 1) < HEAD_DIM
    swapped = pltpu.roll(blk, HEAD_DIM, 1)
    zero = jnp.zeros_like(blk)
    a_ref[rows, col:col + LANES] = jnp.where(lo, blk, zero).astype(BF16)
    b_ref[rows, col:col + LANES] = jnp.where(lo, zero, swapped).astype(BF16)
    a_ref[rows, col + LANES:col + 2 * LANES] = jnp.where(lo, swapped, zero).astype(BF16)
    b_ref[rows, col + LANES:col + 2 * LANES] = jnp.where(lo, zero, blk).astype(BF16)


N_QKV_IN = 7
N_QKV_OUT = 7


def _qkv_kernel(*refs, q_dim, kv_dim, nchunk):
    x_ref, g_ref, w_ref, b_ref, c_ref, sa_ref, sb_ref = refs[:N_QKV_IN]
    n_cast = (len(refs) - N_QKV_IN - N_QKV_OUT) // 2
    cast_in = refs[N_QKV_IN:N_QKV_IN + n_cast]
    q_ref, k_ref, v_ref, ka_ref, kb_ref, va_ref, vb_ref = refs[N_QKV_IN + n_cast:N_QKV_IN + n_cast + N_QKV_OUT]
    cast_out = refs[N_QKV_IN + n_cast + N_QKV_OUT:]
    _run_casts(cast_in, cast_out)

    scale = 1.0 / math.sqrt(HEAD_DIM)
    half = ROT_DIM // 2
    heads_per_blk = LANES // HEAD_DIM
    rot_cols = q_dim + kv_dim
    total = q_dim + 2 * kv_dim
    for r0 in range(0, x_ref.shape[0], SUB_ROWS):
        rows = slice(r0, r0 + SUB_ROWS)
        h = _rms(x_ref[rows, :], g_ref[...]).astype(BF16)
        c, sa, sb = c_ref[rows, :], sa_ref[rows, :], sb_ref[rows, :]
        cq, saq, sbq = c * scale, sa * scale, sb * scale
        for n0 in range(0, total, nchunk):
            acc = jnp.dot(h, w_ref[:, n0:n0 + nchunk], preferred_element_type=F32) + b_ref[:, n0:n0 + nchunk]
            for j0 in range(0, nchunk, LANES):
                col = n0 + j0
                blk = acc[:, j0:j0 + LANES]
                if col < q_dim:
                    blk = blk * cq + pltpu.roll(blk, LANES - half, 1) * saq + pltpu.roll(blk, half, 1) * sbq
                    q_ref[rows, col:col + LANES] = blk.astype(BF16)
                elif col < rot_cols:
                    blk = blk * c + pltpu.roll(blk, LANES - half, 1) * sa + pltpu.roll(blk, half, 1) * sb
                    k_ref[rows, col - q_dim:col - q_dim + LANES] = blk
                    _store_pair_layouts(blk, ka_ref, kb_ref, rows, (col - q_dim) * heads_per_blk)
                else:
                    v_ref[rows, col - rot_cols:col - rot_cols + LANES] = blk
                    _store_pair_layouts(blk, va_ref, vb_ref, rows, (col - rot_cols) * heads_per_blk)


def _qkv_rope(x, g, w_bf16, b, tables, *, tm, q_dim, kv_dim, casts=()):
    m, d = x.shape
    total = q_dim + 2 * kv_dim
    ptiles = tables[0].shape[0] // tm
    tab_spec = pl.BlockSpec((tm, LANES), lambda i: (i % ptiles, 0))
    kern = functools.partial(_qkv_kernel, q_dim=q_dim, kv_dim=kv_dim, nchunk=256)
    pair_w = kv_dim * (LANES // HEAD_DIM)
    c_in, c_out, c_shapes, c_ops = _cast_specs(casts, m // tm, lambda i: i)

    def rows(width):
        return pl.BlockSpec((tm, width), lambda i: (i, 0))

    return pl.pallas_call(
        kern,
        grid=(m // tm,),
        in_specs=[
            rows(d),
            _resident((1, d)),
            _resident((d, total)),
            _resident((1, total)),
            tab_spec, tab_spec, tab_spec,
        ] + c_in,
        out_specs=[rows(q_dim), rows(kv_dim), rows(kv_dim), rows(pair_w), rows(pair_w), rows(pair_w), rows(pair_w)]
        + c_out,
        out_shape=[
            jax.ShapeDtypeStruct((m, q_dim), BF16),
            jax.ShapeDtypeStruct((m, kv_dim), F32),
            jax.ShapeDtypeStruct((m, kv_dim), F32),
        ] + [jax.ShapeDtypeStruct((m, pair_w), BF16)] * 4 + c_shapes,
        compiler_params=_params("parallel"),
        name="qkv_rope",
    )(x, g, w_bf16, b, *tables, *c_ops)


def _chunks_per_body(chunks):
    return next(g for g in (4, 2, 1) if chunks % g == 0)


def _attn_kernel(sink_ref, q_ref, ka_ref, kb_ref, va_ref, vb_ref, o_ref, s_scr, *, n_kv, chunks, group, hist_chunks):
    step = pl.program_id(1)
    span = WINDOW + CHUNK
    pairs = GQA_GROUP // 2
    win_chunks = WINDOW // CHUNK
    neg_inf = -jnp.inf
    row_k = lax.broadcasted_iota(jnp.int32, (2 * span, LANES), 0)
    lane_k = lax.broadcasted_iota(jnp.int32, (2 * span, LANES), 1)
    ones_ab = jnp.where((row_k < span) == (lane_k < HEAD_DIM), 1.0, 0.0).astype(BF16)
    lo = lax.broadcasted_iota(jnp.int32, (CHUNK, LANES), 1) < HEAD_DIM
    nt = (((1,), (1,)), ((), ()))

    def body(j, carry, *, masked):
        units = []
        for u in range(group):
            ci = j * group + u
            ckv = step * chunks + ci + hist_chunks
            k0 = pl.multiple_of(jnp.maximum(ckv - win_chunks, 0) * CHUNK, CHUNK)
            units.append((pl.ds(pl.multiple_of(ci * CHUNK, CHUNK), CHUNK), pl.ds(k0, span), ckv))

        for u, (rows, win, _) in enumerate(units):
            for kh in range(n_kv):
                kcols = slice(kh * LANES, (kh + 1) * LANES)
                q_pairs = jnp.concatenate(
                    [q_ref[rows, (kh * pairs + p) * LANES:(kh * pairs + p + 1) * LANES] for p in range(pairs)], axis=0)
                k_ab = jnp.concatenate([ka_ref[win, kcols], kb_ref[win, kcols]], axis=0)
                s_scr[u, kh] = lax.dot_general(q_pairs, k_ab, nt, preferred_element_type=F32)

        e_all, t_all = [], []
        for u, (_, _, ckv) in enumerate(units):
            if masked:
                key_row = lax.broadcasted_iota(jnp.int32, (1, span), 1)
                bias1 = jnp.where(key_row < (ckv + 1) * CHUNK, 0.0, neg_inf)
                bias = jnp.concatenate([bias1, bias1], axis=1)
            for kh in range(n_kv):
                e_rows, t_rows = [], []
                for p in range(pairs):
                    s = s_scr[u, kh, p * CHUNK:(p + 1) * CHUNK, :]
                    if masked:
                        s = s + bias
                    t0, t1, t2 = s[:, :LANES], s[:, LANES:2 * LANES], s[:, 2 * LANES:]
                    sk_a = sink_ref[kh * GQA_GROUP + 2 * p]
                    sk_b = sink_ref[kh * GQA_GROUP + 2 * p + 1]
                    m_a = jnp.max(jnp.maximum(t0, jnp.where(lo, t1, neg_inf)), axis=-1, keepdims=True)
                    m_b = jnp.max(jnp.maximum(t2, jnp.where(lo, neg_inf, t1)), axis=-1, keepdims=True)
                    m_a, m_b = jnp.maximum(m_a, sk_a), jnp.maximum(m_b, sk_b)
                    m_ab = jnp.where(lo, m_a, m_b)
                    e = jnp.concatenate([jnp.exp(t0 - m_a), jnp.exp(t1 - m_ab), jnp.exp(t2 - m_b)], axis=1)
                    e_rows.append(e.astype(BF16))
                    t_rows.append(jnp.exp(jnp.where(lo[:1], sk_a, sk_b) - m_ab))
                e_all.append(jnp.concatenate(e_rows, axis=0))
                t_all.append(jnp.concatenate(t_rows, axis=0))

        for u, (rows, win, _) in enumerate(units):
            for kh in range(n_kv):
                kcols = slice(kh * LANES, (kh + 1) * LANES)
                v_ab = jnp.concatenate([va_ref[win, kcols], vb_ref[win, kcols]], axis=0)
                rhs = jnp.concatenate([v_ab, ones_ab], axis=1)
                acc = jnp.dot(e_all[u * n_kv + kh], rhs, preferred_element_type=F32)
                o_pairs = (acc[:, :LANES] / (acc[:, LANES:] + t_all[u * n_kv + kh])).astype(BF16)
                for p in range(pairs):
                    col = (kh * pairs + p) * LANES
                    o_ref[rows, col:col + LANES] = o_pairs[p * CHUNK:(p + 1) * CHUNK, :]
        return carry

    def run(masked):
        lax.fori_loop(0, chunks // group, functools.partial(body, masked=masked), 0)

    if hist_chunks >= win_chunks:
        run(False)
    else:
        needs_mask = step * chunks + hist_chunks < win_chunks
        pl.when(needs_mask)(lambda: run(True))
        pl.when(jnp.logical_not(needs_mask))(lambda: run(False))


def _attention(q, ka, kb, va, vb, sinks, *, tq):
    b, s, q_dim = q.shape
    rows, kvw = ka.shape[1:]
    n_kv = kvw // LANES
    span = WINDOW + CHUNK
    chunks = tq // CHUNK
    assert rows >= span and (rows - s) % CHUNK == 0
    hist_spec = pl.BlockSpec((None, rows, kvw), lambda i, c: (i, 0, 0))
    group = _chunks_per_body(chunks)
    kern = functools.partial(_attn_kernel, n_kv=n_kv, chunks=chunks, group=group, hist_chunks=(rows - s) // CHUNK)
    return pl.pallas_call(
        kern,
        grid=(b, s // tq),
        in_specs=[
            pl.BlockSpec(memory_space=pltpu.SMEM),
            pl.BlockSpec((None, tq, q_dim), lambda i, c: (i, c, 0)),
            hist_spec, hist_spec, hist_spec, hist_spec,
        ],
        out_specs=pl.BlockSpec((None, tq, q_dim), lambda i, c: (i, c, 0)),
        out_shape=jax.ShapeDtypeStruct((b, s, q_dim), BF16),
        scratch_shapes=[pltpu.VMEM((group, n_kv, (GQA_GROUP // 2) * CHUNK, 2 * span), F32)],
        compiler_params=_params("parallel", "arbitrary"),
        name="swa_attention",
    )(sinks, q, ka, kb, va, vb)


def _pair_layouts(x3, n_kv):
    b, r, _ = x3.shape
    x4 = x3.astype(BF16).reshape(b, r, n_kv, HEAD_DIM)
    lo = jnp.pad(x4, ((0, 0), (0, 0), (0, 0), (0, LANES - HEAD_DIM)))
    hi = jnp.pad(x4, ((0, 0), (0, 0), (0, 0), (LANES - HEAD_DIM, 0)))
    return lo.reshape(b, r, n_kv * LANES), hi.reshape(b, r, n_kv * LANES)


def _oproj_kernel(o_ref, x_ref, w_ref, g_ref, y_ref):
    mix = jnp.dot(o_ref[...], w_ref[...], preferred_element_type=F32)
    y_ref[...] = x_ref[...] + _rms(mix, g_ref[...])


def _out_proj(o, x, w_bf16, g, *, tm):
    m, d = x.shape
    return pl.pallas_call(
        _oproj_kernel,
        grid=(m // tm,),
        in_specs=[
            pl.BlockSpec((tm, o.shape[1]), lambda i: (i, 0)),
            pl.BlockSpec((tm, d), lambda i: (i, 0)),
            _resident(w_bf16.shape),
            _resident((1, d)),
        ],
        out_specs=pl.BlockSpec((tm, d), lambda i: (i, 0)),
        out_shape=jax.ShapeDtypeStruct((m, d), F32),
        compiler_params=_params("parallel"),
        name="attn_out_proj",
    )(o, x, w_bf16, g)


def _pool_group(h, prev, pos, w_ref, gi):
    w = POOL_WINDOWS[gi]
    group_dim = h.shape[1] // len(POOL_WINDOWS)
    sl = slice(gi * group_dim, (gi + 1) * group_dim)
    hg = h[:, sl]
    acc = jnp.concatenate([prev[:, sl], hg], axis=0)
    shift = 1
    while shift < w:
        acc = acc + pltpu.roll(acc, shift, 0)
        shift *= 2
    cnt = jnp.minimum(pos + 1, w).astype(F32)
    dgrp = (acc[HALO:, :] / cnt - hg).astype(BF16)
    return jnp.dot(dgrp, w_ref[gi], preferred_element_type=F32)


def _pool_mix(h, prev, pos, w_ref, scale):
    outs = [_pool_group(h, prev, pos, w_ref, gi) for gi in range(len(POOL_WINDOWS))]
    return jnp.concatenate(outs, axis=1) * scale


def _pool_kernel(x_ref, hist_ref, gpre_ref, gpost_ref, w_ref, scale_ref, y_ref, tail_ref, *, pos0):
    x = x_ref[...]
    rows = x.shape[0]
    h = _rms(x, gpre_ref[...])
    tail_ref[...] = h[rows - HALO:, :]
    pos = pos0 + lax.broadcasted_iota(jnp.int32, (rows, 1), 0)
    mix = _pool_mix(h, hist_ref[...], pos, w_ref, scale_ref[...])
    y_ref[...] = x + _rms(mix, gpost_ref[...])


def _pool_mixer(x, hist, gpre, gpost, w_bf16, scale, *, pos0):
    b, t, d = x.shape
    return pl.pallas_call(
        functools.partial(_pool_kernel, pos0=pos0),
        grid=(b,),
        in_specs=[
            pl.BlockSpec((None, t, d), lambda bi: (bi, 0, 0)),
            pl.BlockSpec((None, HALO, d), lambda bi: (bi, 0, 0)),
            _resident((1, d)),
            _resident((1, d)),
            _resident(w_bf16.shape),
            _resident((1, d)),
        ],
        out_specs=[
            pl.BlockSpec((None, t, d), lambda bi: (bi, 0, 0)),
            pl.BlockSpec((None, HALO, d), lambda bi: (bi, 0, 0)),
        ],
        out_shape=[
            jax.ShapeDtypeStruct((b, t, d), F32),
            jax.ShapeDtypeStruct((b, HALO, d), F32),
        ],
        compiler_params=_params("parallel"),
        name="pool_mixer",
    )(x, hist, gpre, gpost, w_bf16, scale)


def _ple_pool_kernel(x_ref, p_ref, wgate_ref, wproj_ref, gpre_ref, gpost_ref, w_ref, scale_ref,
                     y_ref, tail_ref, even_scr, odd_scr, halo_scr, *, tm, tiles_per_seq):
    k = pl.program_id(0)

    @pl.when(k == 0)
    def _():
        odd_scr[...] = jnp.zeros_like(odd_scr)
        halo_scr[...] = jnp.zeros_like(halo_scr)

    def step(dst_scr, src_scr):
        x = x_ref[...]
        xb, pb = x.astype(BF16), p_ref[...].astype(BF16)
        n_groups = len(POOL_WINDOWS)
        n_chunks = 2 * n_groups
        cw = x.shape[1] // n_chunks

        def embed_chunk(c):
            cols = slice(c * cw, (c + 1) * cw)
            gate = jax.nn.sigmoid(jnp.dot(xb, wgate_ref[:, cols], preferred_element_type=F32))
            emb = jnp.dot(pb, wproj_ref[:, cols], preferred_element_type=F32)
            dst_scr[:, cols] = x[:, cols] + gate * emb

        embed_chunk(0)
        seq_tile = lax.rem(jnp.maximum(k - 1, 0), tiles_per_seq)
        x1 = src_scr[...]
        h = _rms(x1, gpre_ref[...])
        prev = jnp.where(seq_tile == 0, 0.0, halo_scr[...])
        pos = seq_tile * tm + lax.broadcasted_iota(jnp.int32, (tm, 1), 0)
        tail = h[tm - HALO:, :]
        halo_scr[...] = tail
        tail_ref[...] = tail
        outs = []
        for gi in range(n_groups):
            outs.append(_pool_group(h, prev, pos, w_ref, gi))
            embed_chunk(gi + 1)
        mix = jnp.concatenate(outs, axis=1) * scale_ref[...]
        y_ref[...] = x1 + _rms(mix, gpost_ref[...])
        for c in range(n_groups + 1, n_chunks):
            embed_chunk(c)

    is_even = lax.rem(k, 2) == 0
    pl.when(is_even)(lambda: step(even_scr, odd_scr))
    pl.when(jnp.logical_not(is_even))(lambda: step(odd_scr, even_scr))


def _ple_pool(x, p, wgate, wproj, gpre, gpost, w_bf16, scale, *, layer, tm):
    b, s, d = x.shape
    nt = s // tm
    n_tiles = b * nt

    def cur(k):
        return jnp.minimum(k, n_tiles - 1)

    def lagged(k):
        return jnp.maximum(k - 1, 0)

    return pl.pallas_call(
        functools.partial(_ple_pool_kernel, tm=tm, tiles_per_seq=nt),
        grid=(n_tiles + 1,),
        in_specs=[
            pl.BlockSpec((None, tm, d), lambda k: (cur(k) // nt, cur(k) % nt, 0)),
            pl.BlockSpec((None, tm, p.shape[2]), lambda k: (layer, cur(k), 0)),
            _resident(wgate.shape),
            _resident(wproj.shape, layer),
            _resident((1, d)),
            _resident((1, d)),
            _resident(w_bf16.shape),
            _resident((1, d)),
        ],
        out_specs=[
            pl.BlockSpec((None, tm, d), lambda k: (lagged(k) // nt, lagged(k) % nt, 0)),
            pl.BlockSpec((None, None, HALO, d), lambda k: (lagged(k) // nt, lagged(k) % nt, 0, 0)),
        ],
        out_shape=[
            jax.ShapeDtypeStruct((b, s, d), F32),
            jax.ShapeDtypeStruct((b, nt, HALO, d), F32),
        ],
        scratch_shapes=[pltpu.VMEM((tm, d), F32), pltpu.VMEM((tm, d), F32), pltpu.VMEM((HALO, d), F32)],
        compiler_params=_params("arbitrary"),
        name="ple_pool",
    )(x, p, wgate, wproj, gpre, gpost, w_bf16, scale)


def _ple_kernel(x_ref, p_ref, wgate_ref, wproj_ref, y_ref):
    for r0 in range(0, x_ref.shape[0], SUB_ROWS):
        rows = slice(r0, r0 + SUB_ROWS)
        x = x_ref[rows, :]
        gate = jax.nn.sigmoid(jnp.dot(x.astype(BF16), wgate_ref[...], preferred_element_type=F32))
        emb = jnp.dot(p_ref[rows, :].astype(BF16), wproj_ref[...], preferred_element_type=F32)
        y_ref[rows, :] = x + gate * emb


def _ple(x, p, wgate, wproj, *, layer, tm):
    m, d = x.shape
    return pl.pallas_call(
        _ple_kernel,
        grid=(m // tm,),
        in_specs=[
            pl.BlockSpec((tm, d), lambda i: (i, 0)),
            pl.BlockSpec((None, tm, p.shape[2]), lambda i: (layer, i, 0)),
            _resident(wgate.shape),
            _resident(wproj.shape, layer),
        ],
        out_specs=pl.BlockSpec((tm, d), lambda i: (i, 0)),
        out_shape=jax.ShapeDtypeStruct((m, d), F32),
        compiler_params=_params("parallel"),
        name="ple",
    )(x, p, wgate, wproj)


N_MLP_IN = 5
N_PLE_IN = 3


def _ffn_kernel(*refs, nf, with_ple):
    x_ref, gpre_ref, gpost_ref, wup_ref, wdown_ref = refs[:N_MLP_IN]
    n_in = N_MLP_IN + (N_PLE_IN if with_ple else 0)
    if with_ple:
        p_ref, wgate_ref, wproj_ref = refs[N_MLP_IN:n_in]
    n_cast = (len(refs) - n_in - 2) // 2
    cast_in = refs[n_in:n_in + n_cast]
    y_ref = refs[n_in + n_cast]
    cast_out = refs[n_in + n_cast + 1:n_in + 2 * n_cast + 1]
    h_scr = refs[-1]
    _run_casts(cast_in, cast_out)
    f = pl.program_id(1)

    def mlp_part(h):
        u = jnp.maximum(jnp.dot(h, wup_ref[...], preferred_element_type=F32), 0.0)
        return jnp.dot((u * u).astype(BF16), wdown_ref[...], preferred_element_type=F32)

    @pl.when(f == 0)
    def _():
        h = _rms(x_ref[...], gpre_ref[...]).astype(BF16)
        h_scr[...] = h
        y_ref[...] = mlp_part(h)

    @pl.when(jnp.logical_and(f > 0, f < nf - 1))
    def _():
        y_ref[...] += mlp_part(h_scr[...])

    @pl.when(f == nf - 1)
    def _():
        x1 = x_ref[...] + _rms(y_ref[...] + mlp_part(h_scr[...]), gpost_ref[...])
        if with_ple:
            emb = jnp.dot(p_ref[...].astype(BF16), wproj_ref[...], preferred_element_type=F32)
            gate = jax.nn.sigmoid(jnp.dot(x1.astype(BF16), wgate_ref[...], preferred_element_type=F32))
            x1 = x1 + gate * emb
        y_ref[...] = x1


def _ffn(x, gpre, gpost, wup, wdown, *, tm, tf, ple=None, casts=()):
    m, d = x.shape
    dff = wup.shape[1]
    nf = dff // tf
    assert nf >= 2
    c_in, c_out, c_shapes, c_ops = _cast_specs(casts, (m // tm) * nf, lambda i, f: i * nf + f)
    ple_specs, ple_ops = [], []
    if ple is not None:
        p, wgate, wproj, layer = ple
        ple_specs = [
            pl.BlockSpec((None, tm, p.shape[2]), lambda i, f: (layer, i, 0)),
            _resident(wgate.shape),
            _resident(wproj.shape, layer),
        ]
        ple_ops = [p, wgate, wproj]
    return pl.pallas_call(
        functools.partial(_ffn_kernel, nf=nf, with_ple=ple is not None),
        grid=(m // tm, nf),
        in_specs=[
            pl.BlockSpec((tm, d), lambda i, f: (i, 0)),
            _resident((1, d)),
            _resident((1, d)),
            pl.BlockSpec((d, tf), lambda i, f: (0, f)),
            pl.BlockSpec((tf, d), lambda i, f: (f, 0)),
        ] + ple_specs + c_in,
        out_specs=[pl.BlockSpec((tm, d), lambda i, f: (i, 0))] + c_out,
        out_shape=[jax.ShapeDtypeStruct((m, d), F32)] + c_shapes,
        scratch_shapes=[pltpu.VMEM((tm, d), BF16)],
        compiler_params=_params("parallel", "arbitrary"),
        name="ffn_ple" if ple is not None else "ffn",
    )(x, gpre, gpost, wup, wdown, *ple_ops, *c_ops)


def _row(v):
    return v.reshape(1, -1)


def kernel(x_prompt, x_sample, cache_k, cache_v, state_pool, p_prompt, p_sample, norm_mix_pre, norm_mix_post, norm_ffn_pre, norm_ffn_post, w_qkv, b_qkv, w_o, sinks, w_pool, pool_scale, w_ffn_up, w_ffn_down, w_ple_proj, w_ple_gate):
    b, s, d = x_prompt.shape
    bs, ts, _ = x_sample.shape
    q_dim = w_o.shape[1]
    kv_dim = (w_qkv.shape[2] - q_dim) // 2
    n_kv = kv_dim // HEAD_DIM
    tm = TILES.rows
    assert s % tm == 0 and (bs * ts) % tm == 0 and tm % ts == 0

    def bf(w):
        return w.astype(BF16)

    xp = x_prompt.reshape(b * s, d)
    xs = x_sample.reshape(bs * ts, d)
    wproj = bf(w_ple_proj)
    pp = p_prompt.reshape(p_prompt.shape[0], b * s, -1)
    ps = p_sample.reshape(p_sample.shape[0], bs * ts, -1)

    def ffn(x, i, wup, wdown, *, tf, p=None, wgate=None, casts=()):
        ple = None if p is None else (p, wgate, wproj, i)
        return _ffn(x, _row(norm_ffn_pre[i]), _row(norm_ffn_post[i]), wup, wdown, tm=tm, tf=tf, ple=ple, casts=casts)

    tab_p = _rope_tables(jnp.arange(s, dtype=jnp.int32))
    tab_s = tuple(jnp.tile(t, (tm // ts, 1)) for t in _rope_tables(PAST_LEN + jnp.arange(ts, dtype=jnp.int32)))
    wqkv, bqkv = bf(w_qkv[0]), _row(b_qkv[0])
    g_pre, g_post = _row(norm_mix_pre[0]), _row(norm_mix_post[0])

    casts0 = ((w_ffn_up, 0), (w_ffn_down, 0), (w_ple_gate, 0), (w_ple_gate, 1), (w_o, 0))
    q_p, k_p, v_p, *rest = _qkv_rope(xp, g_pre, wqkv, bqkv, tab_p, tm=tm, q_dim=q_dim, kv_dim=kv_dim, casts=casts0)
    kv_p, (wup0, wdown0, wgate0, wgate1, wo) = rest[:4], rest[4:]
    q_s, k_s, v_s, *kv_s = _qkv_rope(xs, g_pre, wqkv, bqkv, tab_s, tm=tm, q_dim=q_dim, kv_dim=kv_dim)

    k_p3, v_p3 = k_p.reshape(b, s, kv_dim), v_p.reshape(b, s, kv_dim)
    k_s3, v_s3 = k_s.reshape(bs, ts, kv_dim), v_s.reshape(bs, ts, kv_dim)
    o_p = _attention(q_p.reshape(b, s, q_dim), *(t.reshape(b, s, -1) for t in kv_p), sinks[0], tq=TILES.attn_rows)
    cache = (*_pair_layouts(cache_k[0].reshape(bs, WINDOW, kv_dim), n_kv),
             *_pair_layouts(cache_v[0].reshape(bs, WINDOW, kv_dim), n_kv))
    kv_s = (jnp.concatenate([c, t.reshape(bs, ts, -1)], axis=1) for c, t in zip(cache, kv_s))
    o_s = _attention(q_s.reshape(bs, ts, q_dim), *kv_s, sinks[0], tq=ts)

    xp = _out_proj(o_p.reshape(b * s, q_dim), xp, wo, g_post, tm=TILES.rows_wide)
    xs = _out_proj(o_s.reshape(bs * ts, q_dim), xs, wo, g_post, tm=tm)

    xp, wup1, wdown1 = ffn(xp, 0, wup0, wdown0, tf=TILES.dff, casts=((w_ffn_up, 1), (w_ffn_down, 1)))
    (xs,) = ffn(xs, 0, wup0, wdown0, tf=TILES.dff_ple, p=ps, wgate=wgate0)

    g_pre, g_post = _row(norm_mix_pre[1]), _row(norm_mix_post[1])
    wpool, pscale = bf(w_pool[0]), _row(pool_scale[0])
    xp3, tail_p = _ple_pool(xp.reshape(b, s, d), pp, wgate0, wproj, g_pre, g_post, wpool, pscale,
                            layer=0, tm=tm)
    hist = jnp.pad(state_pool[0], ((0, 0), (HALO - POOL_HIST, 0), (0, 0)))
    xs3, tail_s = _pool_mixer(xs.reshape(bs, ts, d), hist, g_pre, g_post, wpool, pscale, pos0=PAST_LEN)

    (xp,) = ffn(xp3.reshape(b * s, d), 1, wup1, wdown1, tf=TILES.dff)
    xp = _ple(xp, pp, wgate1, wproj, layer=1, tm=TILES.rows_wide)
    (xs,) = ffn(xs3.reshape(bs * ts, d), 1, wup1, wdown1, tf=TILES.dff_ple, p=ps, wgate=wgate1)

    n_heads_kv = (n_kv, HEAD_DIM)
    return (
        xp.reshape(b, s, d),
        xs.reshape(bs, ts, d),
        k_p3[:, s - WINDOW:].reshape(1, b, WINDOW, *n_heads_kv),
        v_p3[:, s - WINDOW:].reshape(1, b, WINDOW, *n_heads_kv),
        tail_p[:, -1, HALO - POOL_HIST:][None],
        k_s3.reshape(1, bs, ts, *n_heads_kv),
        v_s3.reshape(1, bs, ts, *n_heads_kv),
        tail_s[:, HALO - POOL_HIST:][None],
    )
```

```python
import functools
import math
from typing import NamedTuple

import jax
import jax.numpy as jnp
from jax import lax
from jax.experimental import pallas as pl
from jax.experimental.pallas import tpu as pltpu

HEAD_DIM = 64
GQA_GROUP = 8
CHUNK = 64
WINDOW = 128
ROT_DIM = HEAD_DIM // 4
ROPE_THETA = 500000.0
POOL_WINDOWS = (2, 4, 8, 16)
POOL_HIST = max(POOL_WINDOWS) - 1
PAST_LEN = 2048
EPS = 1e-6
LOG2E = math.log2(math.e)

LANES = 128
BF16_ROWS = 16
HALO = 16
SUB_ROWS = 256
VMEM_LIMIT_BYTES = 60 * 1024 * 1024


class _Tiles(NamedTuple):
    rows: int = 512
    rows_wide: int = 1024
    attn_rows: int = 512
    dff: int = 2048
    dff_ple: int = 1024


TILES = _Tiles()

F32 = jnp.float32
BF16 = jnp.bfloat16


def _rms(x, g):
    ms = jnp.mean(x * x, axis=-1, keepdims=True)
    return x * lax.rsqrt(ms + EPS) * g


def _resident(shape, layer=None):
    if layer is None:
        nd = len(shape)
        return pl.BlockSpec(shape, lambda *_: (0,) * nd, pipeline_mode=pl.Buffered(1))
    nd = len(shape) - 1
    return pl.BlockSpec((None,) + tuple(shape[1:]), lambda *_: (layer,) + (0,) * nd, pipeline_mode=pl.Buffered(1))


def _params(*sem):
    return pltpu.CompilerParams(dimension_semantics=sem, vmem_limit_bytes=VMEM_LIMIT_BYTES)


def _cast_specs(casts, n_steps, flat_step):
    in_specs, out_specs, out_shapes, operands = [], [], [], []
    for w, layer in casts:
        _, r, c = w.shape
        rb = max(BF16_ROWS, r // n_steps)
        csplit = rb * n_steps // r
        assert r % rb == 0 and (r // rb) * csplit == n_steps and c % (csplit * LANES) == 0
        cw = c // csplit

        def in_map(*idx, layer=layer, csplit=csplit):
            k = flat_step(*idx)
            return (layer, k // csplit, k % csplit)

        def out_map(*idx, csplit=csplit):
            k = flat_step(*idx)
            return (k // csplit, k % csplit)

        in_specs.append(pl.BlockSpec((None, rb, cw), in_map))
        out_specs.append(pl.BlockSpec((rb, cw), out_map))
        out_shapes.append(jax.ShapeDtypeStruct((r, c), BF16))
        operands.append(w)
    return in_specs, out_specs, out_shapes, operands


def _run_casts(srcs, dsts):
    for src, dst in zip(srcs, dsts, strict=True):
        dst[...] = src[...].astype(BF16)


def _rope_tables(positions):
    half = ROT_DIM // 2
    inv = ROPE_THETA ** (-jnp.arange(0, ROT_DIM, 2, dtype=F32) / ROT_DIM)
    ang = positions.astype(F32)[:, None] * inv[None, :]
    cos, sin = jnp.cos(ang), jnp.sin(ang)
    p = positions.shape[0]
    ones = jnp.ones((p, HEAD_DIM - ROT_DIM), F32)
    zeros = jnp.zeros((p, HEAD_DIM - ROT_DIM), F32)
    zh = jnp.zeros((p, half), F32)
    c = jnp.concatenate([cos, cos, ones], axis=1)
    sa = jnp.concatenate([-sin, zh, zeros], axis=1)
    sb = jnp.concatenate([zh, sin, zeros], axis=1)
    rep = LANES // HEAD_DIM
    return tuple(jnp.tile(t, (1, rep)) for t in (c, sa, sb))


def _store_pair_layouts(blk, a_ref, b_ref, rows, col):
    lo = lax.broadcasted_iota(jnp.int32, blk.shape, 1) < HEAD_DIM
    swapped = pltpu.roll(blk, HEAD_DIM, 1)
    zero = jnp.zeros_like(blk)
    a_ref[rows, col:col + LANES] = jnp.where(lo, blk, zero).astype(BF16)
    b_ref[rows, col:col + LANES] = jnp.where(lo, zero, swapped).astype(BF16)
    a_ref[rows, col + LANES:col + 2 * LANES] = jnp.where(lo, swapped, zero).astype(BF16)
    b_ref[rows, col + LANES:col + 2 * LANES] = jnp.where(lo, zero, blk).astype(BF16)


N_QKV_IN = 7
N_QKV_OUT = 7


def _qkv_kernel(*refs, q_dim, kv_dim, nchunk):
    x_ref, g_ref, w_ref, b_ref, c_ref, sa_ref, sb_ref = refs[:N_QKV_IN]
    n_cast = (len(refs) - N_QKV_IN - N_QKV_OUT) // 2
    cast_in = refs[N_QKV_IN:N_QKV_IN + n_cast]
    q_ref, k_ref, v_ref, ka_ref, kb_ref, va_ref, vb_ref = refs[N_QKV_IN + n_cast:N_QKV_IN + n_cast + N_QKV_OUT]
    cast_out = refs[N_QKV_IN + n_cast + N_QKV_OUT:]
    _run_casts(cast_in, cast_out)

    scale = LOG2E / math.sqrt(HEAD_DIM)
    half = ROT_DIM // 2
    heads_per_blk = LANES // HEAD_DIM
    rot_cols = q_dim + kv_dim
    total = q_dim + 2 * kv_dim
    for r0 in range(0, x_ref.shape[0], SUB_ROWS):
        rows = slice(r0, r0 + SUB_ROWS)
        h = _rms(x_ref[rows, :], g_ref[...]).astype(BF16)
        c, sa, sb = c_ref[rows, :], sa_ref[rows, :], sb_ref[rows, :]
        cq, saq, sbq = c * scale, sa * scale, sb * scale
        for n0 in range(0, total, nchunk):
            acc = jnp.dot(h, w_ref[:, n0:n0 + nchunk], preferred_element_type=F32) + b_ref[:, n0:n0 + nchunk]
            for j0 in range(0, nchunk, LANES):
                col = n0 + j0
                blk = acc[:, j0:j0 + LANES]
                if col < q_dim:
                    blk = blk * cq + pltpu.roll(blk, LANES - half, 1) * saq + pltpu.roll(blk, half, 1) * sbq
                    q_ref[rows, col:col + LANES] = blk.astype(BF16)
                elif col < rot_cols:
                    blk = blk * c + pltpu.roll(blk, LANES - half, 1) * sa + pltpu.roll(blk, half, 1) * sb
                    k_ref[rows, col - q_dim:col - q_dim + LANES] = blk
                    _store_pair_layouts(blk, ka_ref, kb_ref, rows, (col - q_dim) * heads_per_blk)
                else:
                    v_ref[rows, col - rot_cols:col - rot_cols + LANES] = blk
                    _store_pair_layouts(blk, va_ref, vb_ref, rows, (col - rot_cols) * heads_per_blk)


def _qkv_rope(x, g, w_bf16, b, tables, *, tm, q_dim, kv_dim, casts=()):
    m, d = x.shape
    total = q_dim + 2 * kv_dim
    ptiles = tables[0].shape[0] // tm
    tab_spec = pl.BlockSpec((tm, LANES), lambda i: (i % ptiles, 0))
    kern = functools.partial(_qkv_kernel, q_dim=q_dim, kv_dim=kv_dim, nchunk=256)
    pair_w = kv_dim * (LANES // HEAD_DIM)
    c_in, c_out, c_shapes, c_ops = _cast_specs(casts, m // tm, lambda i: i)

    def rows(width):
        return pl.BlockSpec((tm, width), lambda i: (i, 0))

    return pl.pallas_call(
        kern,
        grid=(m // tm,),
        in_specs=[
            rows(d),
            _resident((1, d)),
            _resident((d, total)),
            _resident((1, total)),
            tab_spec, tab_spec, tab_spec,
        ] + c_in,
        out_specs=[rows(q_dim), rows(kv_dim), rows(kv_dim), rows(pair_w), rows(pair_w), rows(pair_w), rows(pair_w)]
        + c_out,
        out_shape=[
            jax.ShapeDtypeStruct((m, q_dim), BF16),
            jax.ShapeDtypeStruct((m, kv_dim), F32),
            jax.ShapeDtypeStruct((m, kv_dim), F32),
        ] + [jax.ShapeDtypeStruct((m, pair_w), BF16)] * 4 + c_shapes,
        compiler_params=_params("parallel"),
        name="qkv_rope",
    )(x, g, w_bf16, b, *tables, *c_ops)


def _chunks_per_body(chunks):
    return next(g for g in (4, 2, 1) if chunks % g == 0)


def _attn_kernel(sink_ref, q_ref, ka_ref, kb_ref, va_ref, vb_ref, o_ref, s_scr, *, n_kv, chunks, group, hist_chunks):
    step = pl.program_id(1)
    span = WINDOW + CHUNK
    pairs = GQA_GROUP // 2
    win_chunks = WINDOW // CHUNK
    neg_inf = -jnp.inf
    row_k = lax.broadcasted_iota(jnp.int32, (2 * span, LANES), 0)
    lane_k = lax.broadcasted_iota(jnp.int32, (2 * span, LANES), 1)
    ones_ab = jnp.where((row_k < span) == (lane_k < HEAD_DIM), 1.0, 0.0).astype(BF16)
    lo = lax.broadcasted_iota(jnp.int32, (CHUNK, LANES), 1) < HEAD_DIM
    nt = (((1,), (1,)), ((), ()))

    def body(j, carry, *, masked):
        units = []
        for u in range(group):
            ci = j * group + u
            ckv = step * chunks + ci + hist_chunks
            k0 = pl.multiple_of(jnp.maximum(ckv - win_chunks, 0) * CHUNK, CHUNK)
            units.append((pl.ds(pl.multiple_of(ci * CHUNK, CHUNK), CHUNK), pl.ds(k0, span), ckv))

        for u, (rows, win, _) in enumerate(units):
            for kh in range(n_kv):
                kcols = slice(kh * LANES, (kh + 1) * LANES)
                q_pairs = jnp.concatenate(
                    [q_ref[rows, (kh * pairs + p) * LANES:(kh * pairs + p + 1) * LANES] for p in range(pairs)], axis=0)
                k_ab = jnp.concatenate([ka_ref[win, kcols], kb_ref[win, kcols]], axis=0)
                s_scr[u, kh] = lax.dot_general(q_pairs, k_ab, nt, preferred_element_type=F32)

        e_all, t_all = [], []
        for u, (_, _, ckv) in enumerate(units):
            if masked:
                key_row = lax.broadcasted_iota(jnp.int32, (1, span), 1)
                bias1 = jnp.where(key_row < (ckv + 1) * CHUNK, 0.0, neg_inf)
                bias = jnp.concatenate([bias1, bias1], axis=1)
            for kh in range(n_kv):
                e_rows, t_rows = [], []
                for p in range(pairs):
                    s = s_scr[u, kh, p * CHUNK:(p + 1) * CHUNK, :]
                    if masked:
                        s = s + bias
                    t0, t1, t2 = s[:, :LANES], s[:, LANES:2 * LANES], s[:, 2 * LANES:]
                    sk_a = sink_ref[kh * GQA_GROUP + 2 * p] * LOG2E
                    sk_b = sink_ref[kh * GQA_GROUP + 2 * p + 1] * LOG2E
                    m_a = jnp.max(jnp.maximum(t0, jnp.where(lo, t1, neg_inf)), axis=-1, keepdims=True)
                    m_b = jnp.max(jnp.maximum(t2, jnp.where(lo, neg_inf, t1)), axis=-1, keepdims=True)
                    m_a, m_b = jnp.maximum(m_a, sk_a), jnp.maximum(m_b, sk_b)
                    m_ab = jnp.where(lo, m_a, m_b)
                    e = jnp.concatenate([jnp.exp2(t0 - m_a), jnp.exp2(t1 - m_ab), jnp.exp2(t2 - m_b)], axis=1)
                    e_rows.append(e.astype(BF16))
                    t_rows.append(jnp.exp2(jnp.where(lo[:1], sk_a, sk_b) - m_ab))
                e_all.append(jnp.concatenate(e_rows, axis=0))
                t_all.append(jnp.concatenate(t_rows, axis=0))

        for u, (rows, win, _) in enumerate(units):
            for kh in range(n_kv):
                kcols = slice(kh * LANES, (kh + 1) * LANES)
                v_ab = jnp.concatenate([va_ref[win, kcols], vb_ref[win, kcols]], axis=0)
                rhs = jnp.concatenate([v_ab, ones_ab], axis=1)
                acc = jnp.dot(e_all[u * n_kv + kh], rhs, preferred_element_type=F32)
                o_pairs = (acc[:, :LANES] / (acc[:, LANES:] + t_all[u * n_kv + kh])).astype(BF16)
                for p in range(pairs):
                    col = (kh * pairs + p) * LANES
                    o_ref[rows, col:col + LANES] = o_pairs[p * CHUNK:(p + 1) * CHUNK, :]
        return carry

    def run(masked):
        lax.fori_loop(0, chunks // group, functools.partial(body, masked=masked), 0)

    if hist_chunks >= win_chunks:
        run(False)
    else:
        needs_mask = step * chunks + hist_chunks < win_chunks
        pl.when(needs_mask)(lambda: run(True))
        pl.when(jnp.logical_not(needs_mask))(lambda: run(False))


def _attention(q, ka, kb, va, vb, sinks, *, tq):
    b, s, q_dim = q.shape
    rows, kvw = ka.shape[1:]
    n_kv = kvw // LANES
    span = WINDOW + CHUNK
    chunks = tq // CHUNK
    assert rows >= span and (rows - s) % CHUNK == 0
    hist_spec = pl.BlockSpec((None, rows, kvw), lambda i, c: (i, 0, 0))
    group = _chunks_per_body(chunks)
    kern = functools.partial(_attn_kernel, n_kv=n_kv, chunks=chunks, group=group, hist_chunks=(rows - s) // CHUNK)
    return pl.pallas_call(
        kern,
        grid=(b, s // tq),
        in_specs=[
            pl.BlockSpec(memory_space=pltpu.SMEM),
            pl.BlockSpec((None, tq, q_dim), lambda i, c: (i, c, 0)),
            hist_spec, hist_spec, hist_spec, hist_spec,
        ],
        out_specs=pl.BlockSpec((None, tq, q_dim), lambda i, c: (i, c, 0)),
        out_shape=jax.ShapeDtypeStruct((b, s, q_dim), BF16),
        scratch_shapes=[pltpu.VMEM((group, n_kv, (GQA_GROUP // 2) * CHUNK, 2 * span), F32)],
        compiler_params=_params("parallel", "arbitrary"),
        name="swa_attention",
    )(sinks, q, ka, kb, va, vb)


def _pair_layouts(x3, n_kv):
    b, r, _ = x3.shape
    x4 = x3.astype(BF16).reshape(b, r, n_kv, HEAD_DIM)
    lo = jnp.pad(x4, ((0, 0), (0, 0), (0, 0), (0, LANES - HEAD_DIM)))
    hi = jnp.pad(x4, ((0, 0), (0, 0), (0, 0), (LANES - HEAD_DIM, 0)))
    return lo.reshape(b, r, n_kv * LANES), hi.reshape(b, r, n_kv * LANES)


def _oproj_kernel(o_ref, x_ref, w_ref, g_ref, y_ref):
    mix = jnp.dot(o_ref[...], w_ref[...], preferred_element_type=F32)
    y_ref[...] = x_ref[...] + _rms(mix, g_ref[...])


def _out_proj(o, x, w_bf16, g, *, tm):
    m, d = x.shape
    return pl.pallas_call(
        _oproj_kernel,
        grid=(m // tm,),
        in_specs=[
            pl.BlockSpec((tm, o.shape[1]), lambda i: (i, 0)),
            pl.BlockSpec((tm, d), lambda i: (i, 0)),
            _resident(w_bf16.shape),
            _resident((1, d)),
        ],
        out_specs=pl.BlockSpec((tm, d), lambda i: (i, 0)),
        out_shape=jax.ShapeDtypeStruct((m, d), F32),
        compiler_params=_params("parallel"),
        name="attn_out_proj",
    )(o, x, w_bf16, g)


def _pool_group(h, prev, pos, w_ref, gi):
    w = POOL_WINDOWS[gi]
    group_dim = h.shape[1] // len(POOL_WINDOWS)
    sl = slice(gi * group_dim, (gi + 1) * group_dim)
    hg = h[:, sl]
    acc = jnp.concatenate([prev[:, sl], hg], axis=0)
    shift = 1
    while shift < w:
        acc = acc + pltpu.roll(acc, shift, 0)
        shift *= 2
    cnt = jnp.minimum(pos + 1, w).astype(F32)
    dgrp = (acc[HALO:, :] / cnt - hg).astype(BF16)
    return jnp.dot(dgrp, w_ref[gi], preferred_element_type=F32)


def _pool_mix(h, prev, pos, w_ref, scale):
    outs = [_pool_group(h, prev, pos, w_ref, gi) for gi in range(len(POOL_WINDOWS))]
    return jnp.concatenate(outs, axis=1) * scale


def _pool_kernel(x_ref, hist_ref, gpre_ref, gpost_ref, w_ref, scale_ref, y_ref, tail_ref, *, pos0):
    x = x_ref[...]
    rows = x.shape[0]
    h = _rms(x, gpre_ref[...])
    tail_ref[...] = h[rows - HALO:, :]
    pos = pos0 + lax.broadcasted_iota(jnp.int32, (rows, 1), 0)
    mix = _pool_mix(h, hist_ref[...], pos, w_ref, scale_ref[...])
    y_ref[...] = x + _rms(mix, gpost_ref[...])


def _pool_mixer(x, hist, gpre, gpost, w_bf16, scale, *, pos0):
    b, t, d = x.shape
    return pl.pallas_call(
        functools.partial(_pool_kernel, pos0=pos0),
        grid=(b,),
        in_specs=[
            pl.BlockSpec((None, t, d), lambda bi: (bi, 0, 0)),
            pl.BlockSpec((None, HALO, d), lambda bi: (bi, 0, 0)),
            _resident((1, d)),
            _resident((1, d)),
            _resident(w_bf16.shape),
            _resident((1, d)),
        ],
        out_specs=[
            pl.BlockSpec((None, t, d), lambda bi: (bi, 0, 0)),
            pl.BlockSpec((None, HALO, d), lambda bi: (bi, 0, 0)),
        ],
        out_shape=[
            jax.ShapeDtypeStruct((b, t, d), F32),
            jax.ShapeDtypeStruct((b, HALO, d), F32),
        ],
        compiler_params=_params("parallel"),
        name="pool_mixer",
    )(x, hist, gpre, gpost, w_bf16, scale)


N_PLE_POOL_IN = 8


def _ple_pool_kernel(*refs, tm, tiles_per_seq, n_tiles):
    x_ref, p_ref, wgate_ref, wproj_ref, gpre_ref, gpost_ref, w_ref, scale_ref = refs[:N_PLE_POOL_IN]
    n_cast = (len(refs) - N_PLE_POOL_IN - 5) // 2
    cast_in = refs[N_PLE_POOL_IN:N_PLE_POOL_IN + n_cast]
    y_ref, tail_ref = refs[N_PLE_POOL_IN + n_cast:N_PLE_POOL_IN + n_cast + 2]
    cast_out = refs[N_PLE_POOL_IN + n_cast + 2:N_PLE_POOL_IN + 2 * n_cast + 2]
    even_scr, odd_scr, halo_scr = refs[-3:]
    k = pl.program_id(0)

    @pl.when(k == 0)
    def _():
        odd_scr[...] = jnp.zeros_like(odd_scr)
        halo_scr[...] = jnp.zeros_like(halo_scr)

    def step(dst_scr, src_scr):
        _run_casts(cast_in, cast_out)
        n_groups = len(POOL_WINDOWS)
        n_chunks = 2 * n_groups
        if dst_scr is not None:
            x = x_ref[...]
            xb, pb = x.astype(BF16), p_ref[...].astype(BF16)
            cw = x.shape[1] // n_chunks

        def embed_chunk(c):
            if dst_scr is None:
                return
            cols = slice(c * cw, (c + 1) * cw)
            gate = jax.nn.sigmoid(jnp.dot(xb, wgate_ref[:, cols], preferred_element_type=F32))
            emb = jnp.dot(pb, wproj_ref[:, cols], preferred_element_type=F32)
            dst_scr[:, cols] = x[:, cols] + gate * emb

        embed_chunk(0)
        seq_tile = lax.rem(jnp.maximum(k - 1, 0), tiles_per_seq)
        x1 = src_scr[...]
        h = _rms(x1, gpre_ref[...])
        prev = jnp.where(seq_tile == 0, 0.0, halo_scr[...])
        pos = seq_tile * tm + lax.broadcasted_iota(jnp.int32, (tm, 1), 0)
        tail = h[tm - HALO:, :]
        halo_scr[...] = tail
        tail_ref[...] = tail
        outs = []
        for gi in range(n_groups):
            outs.append(_pool_group(h, prev, pos, w_ref, gi))
            embed_chunk(gi + 1)
        mix = jnp.concatenate(outs, axis=1) * scale_ref[...]
        y_ref[...] = x1 + _rms(mix, gpost_ref[...])
        for c in range(n_groups + 1, n_chunks):
            embed_chunk(c)

    is_even = lax.rem(k, 2) == 0
    is_last = k == n_tiles
    last_src = even_scr if (n_tiles - 1) % 2 == 0 else odd_scr
    pl.when(jnp.logical_and(is_even, jnp.logical_not(is_last)))(lambda: step(even_scr, odd_scr))
    pl.when(jnp.logical_and(jnp.logical_not(is_even), jnp.logical_not(is_last)))(lambda: step(odd_scr, even_scr))
    pl.when(is_last)(lambda: step(None, last_src))


def _ple_pool(x, p, wgate, wproj, gpre, gpost, w_bf16, scale, *, layer, tm, casts=()):
    b, s, d = x.shape
    nt = s // tm
    n_tiles = b * nt
    c_in, c_out, c_shapes, c_ops = _cast_specs(casts, n_tiles, lambda k: jnp.minimum(k, n_tiles - 1))

    def cur(k):
        return jnp.minimum(k, n_tiles - 1)

    def lagged(k):
        return jnp.maximum(k - 1, 0)

    return pl.pallas_call(
        functools.partial(_ple_pool_kernel, tm=tm, tiles_per_seq=nt, n_tiles=n_tiles),
        grid=(n_tiles + 1,),
        in_specs=[
            pl.BlockSpec((None, tm, d), lambda k: (cur(k) // nt, cur(k) % nt, 0)),
            pl.BlockSpec((None, tm, p.shape[2]), lambda k: (layer, cur(k), 0)),
            _resident(wgate.shape),
            _resident(wproj.shape, layer),
            _resident((1, d)),
            _resident((1, d)),
            _resident(w_bf16.shape),
            _resident((1, d)),
        ] + c_in,
        out_specs=[
            pl.BlockSpec((None, tm, d), lambda k: (lagged(k) // nt, lagged(k) % nt, 0)),
            pl.BlockSpec((None, None, HALO, d), lambda k: (lagged(k) // nt, lagged(k) % nt, 0, 0)),
        ] + c_out,
        out_shape=[
            jax.ShapeDtypeStruct((b, s, d), F32),
            jax.ShapeDtypeStruct((b, nt, HALO, d), F32),
        ] + c_shapes,
        scratch_shapes=[pltpu.VMEM((tm, d), F32), pltpu.VMEM((tm, d), F32), pltpu.VMEM((HALO, d), F32)],
        compiler_params=_params("arbitrary"),
        name="ple_pool",
    )(x, p, wgate, wproj, gpre, gpost, w_bf16, scale, *c_ops)


def _ple_kernel(x_ref, p_ref, wgate_ref, wproj_ref, y_ref):
    for r0 in range(0, x_ref.shape[0], SUB_ROWS):
        rows = slice(r0, r0 + SUB_ROWS)
        x = x_ref[rows, :]
        gate = jax.nn.sigmoid(jnp.dot(x.astype(BF16), wgate_ref[...], preferred_element_type=F32))
        emb = jnp.dot(p_ref[rows, :].astype(BF16), wproj_ref[...], preferred_element_type=F32)
        y_ref[rows, :] = x + gate * emb


def _ple(x, p, wgate, wproj, *, layer, tm):
    m, d = x.shape
    return pl.pallas_call(
        _ple_kernel,
        grid=(m // tm,),
        in_specs=[
            pl.BlockSpec((tm, d), lambda i: (i, 0)),
            pl.BlockSpec((None, tm, p.shape[2]), lambda i: (layer, i, 0)),
            _resident(wgate.shape),
            _resident(wproj.shape, layer),
        ],
        out_specs=pl.BlockSpec((tm, d), lambda i: (i, 0)),
        out_shape=jax.ShapeDtypeStruct((m, d), F32),
        compiler_params=_params("parallel"),
        name="ple",
    )(x, p, wgate, wproj)


N_MLP_IN = 5
N_PLE_IN = 3


def _ffn_kernel(*refs, nf, with_ple):
    x_ref, gpre_ref, gpost_ref, wup_ref, wdown_ref = refs[:N_MLP_IN]
    n_in = N_MLP_IN + (N_PLE_IN if with_ple else 0)
    if with_ple:
        p_ref, wgate_ref, wproj_ref = refs[N_MLP_IN:n_in]
    n_cast = (len(refs) - n_in - 2) // 2
    cast_in = refs[n_in:n_in + n_cast]
    y_ref = refs[n_in + n_cast]
    cast_out = refs[n_in + n_cast + 1:n_in + 2 * n_cast + 1]
    h_scr = refs[-1]
    _run_casts(cast_in, cast_out)
    f = pl.program_id(1)

    def mlp_part(h):
        u = jnp.maximum(jnp.dot(h, wup_ref[...], preferred_element_type=F32), 0.0)
        return jnp.dot((u * u).astype(BF16), wdown_ref[...], preferred_element_type=F32)

    @pl.when(f == 0)
    def _():
        h = _rms(x_ref[...], gpre_ref[...]).astype(BF16)
        h_scr[...] = h
        y_ref[...] = mlp_part(h)

    @pl.when(jnp.logical_and(f > 0, f < nf - 1))
    def _():
        y_ref[...] += mlp_part(h_scr[...])

    @pl.when(f == nf - 1)
    def _():
        x1 = x_ref[...] + _rms(y_ref[...] + mlp_part(h_scr[...]), gpost_ref[...])
        if with_ple:
            emb = jnp.dot(p_ref[...].astype(BF16), wproj_ref[...], preferred_element_type=F32)
            gate = jax.nn.sigmoid(jnp.dot(x1.astype(BF16), wgate_ref[...], preferred_element_type=F32))
            x1 = x1 + gate * emb
        y_ref[...] = x1


def _ffn(x, gpre, gpost, wup, wdown, *, tm, tf, ple=None, casts=()):
    m, d = x.shape
    dff = wup.shape[1]
    nf = dff // tf
    assert nf >= 2
    c_in, c_out, c_shapes, c_ops = _cast_specs(casts, (m // tm) * nf, lambda i, f: i * nf + f)
    ple_specs, ple_ops = [], []
    if ple is not None:
        p, wgate, wproj, layer = ple
        ple_specs = [
            pl.BlockSpec((None, tm, p.shape[2]), lambda i, f: (layer, i, 0)),
            _resident(wgate.shape),
            _resident(wproj.shape, layer),
        ]
        ple_ops = [p, wgate, wproj]
    return pl.pallas_call(
        functools.partial(_ffn_kernel, nf=nf, with_ple=ple is not None),
        grid=(m // tm, nf),
        in_specs=[
            pl.BlockSpec((tm, d), lambda i, f: (i, 0)),
            _resident((1, d)),
            _resident((1, d)),
            pl.BlockSpec((d, tf), lambda i, f: (0, f)),
            pl.BlockSpec((tf, d), lambda i, f: (f, 0)),
        ] + ple_specs + c_in,
        out_specs=[pl.BlockSpec((tm, d), lambda i, f: (i, 0))] + c_out,
        out_shape=[jax.ShapeDtypeStruct((m, d), F32)] + c_shapes,
        scratch_shapes=[pltpu.VMEM((tm, d), BF16)],
        compiler_params=_params("parallel", "arbitrary"),
        name="ffn_ple" if ple is not None else "ffn",
    )(x, gpre, gpost, wup, wdown, *ple_ops, *c_ops)


def _row(v):
    return v.reshape(1, -1)


def kernel(x_prompt, x_sample, cache_k, cache_v, state_pool, p_prompt, p_sample, norm_mix_pre, norm_mix_post, norm_ffn_pre, norm_ffn_post, w_qkv, b_qkv, w_o, sinks, w_pool, pool_scale, w_ffn_up, w_ffn_down, w_ple_proj, w_ple_gate):
    b, s, d = x_prompt.shape
    bs, ts, _ = x_sample.shape
    q_dim = w_o.shape[1]
    kv_dim = (w_qkv.shape[2] - q_dim) // 2
    n_kv = kv_dim // HEAD_DIM
    tm = TILES.rows
    assert s % tm == 0 and (bs * ts) % tm == 0 and tm % ts == 0

    def bf(w):
        return w.astype(BF16)

    xp = x_prompt.reshape(b * s, d)
    xs = x_sample.reshape(bs * ts, d)
    wproj = bf(w_ple_proj)
    pp = p_prompt.reshape(p_prompt.shape[0], b * s, -1)
    ps = p_sample.reshape(p_sample.shape[0], bs * ts, -1)

    def ffn(x, i, wup, wdown, *, tf, p=None, wgate=None, casts=()):
        ple = None if p is None else (p, wgate, wproj, i)
        return _ffn(x, _row(norm_ffn_pre[i]), _row(norm_ffn_post[i]), wup, wdown, tm=tm, tf=tf, ple=ple, casts=casts)

    tab_p = _rope_tables(jnp.arange(s, dtype=jnp.int32))
    tab_s = tuple(jnp.tile(t, (tm // ts, 1)) for t in _rope_tables(PAST_LEN + jnp.arange(ts, dtype=jnp.int32)))
    wqkv, bqkv = bf(w_qkv[0]), _row(b_qkv[0])
    g_pre, g_post = _row(norm_mix_pre[0]), _row(norm_mix_post[0])

    casts0 = ((w_ffn_up, 0), (w_ffn_down, 0), (w_ple_gate, 0), (w_ple_gate, 1), (w_o, 0))
    q_p, k_p, v_p, *rest = _qkv_rope(xp, g_pre, wqkv, bqkv, tab_p, tm=tm, q_dim=q_dim, kv_dim=kv_dim, casts=casts0)
    kv_p, (wup0, wdown0, wgate0, wgate1, wo) = rest[:4], rest[4:]
    q_s, k_s, v_s, *kv_s = _qkv_rope(xs, g_pre, wqkv, bqkv, tab_s, tm=tm, q_dim=q_dim, kv_dim=kv_dim)

    k_p3, v_p3 = k_p.reshape(b, s, kv_dim), v_p.reshape(b, s, kv_dim)
    k_s3, v_s3 = k_s.reshape(bs, ts, kv_dim), v_s.reshape(bs, ts, kv_dim)
    o_p = _attention(q_p.reshape(b, s, q_dim), *(t.reshape(b, s, -1) for t in kv_p), sinks[0], tq=TILES.attn_rows)
    cache = (*_pair_layouts(cache_k[0].reshape(bs, WINDOW, kv_dim), n_kv),
             *_pair_layouts(cache_v[0].reshape(bs, WINDOW, kv_dim), n_kv))
    kv_s = (jnp.concatenate([c, t.reshape(bs, ts, -1)], axis=1) for c, t in zip(cache, kv_s))
    o_s = _attention(q_s.reshape(bs, ts, q_dim), *kv_s, sinks[0], tq=ts)

    xp = _out_proj(o_p.reshape(b * s, q_dim), xp, wo, g_post, tm=TILES.rows_wide)
    xs = _out_proj(o_s.reshape(bs * ts, q_dim), xs, wo, g_post, tm=tm)

    (xp,) = ffn(xp, 0, wup0, wdown0, tf=TILES.dff)
    (xs,) = ffn(xs, 0, wup0, wdown0, tf=TILES.dff_ple, p=ps, wgate=wgate0)

    g_pre, g_post = _row(norm_mix_pre[1]), _row(norm_mix_post[1])
    wpool, pscale = bf(w_pool[0]), _row(pool_scale[0])
    xp3, tail_p, wup1, wdown1 = _ple_pool(xp.reshape(b, s, d), pp, wgate0, wproj, g_pre, g_post, wpool, pscale,
                                          layer=0, tm=tm, casts=((w_ffn_up, 1), (w_ffn_down, 1)))
    hist = jnp.pad(state_pool[0], ((0, 0), (HALO - POOL_HIST, 0), (0, 0)))
    xs3, tail_s = _pool_mixer(xs.reshape(bs, ts, d), hist, g_pre, g_post, wpool, pscale, pos0=PAST_LEN)

    (xp,) = ffn(xp3.reshape(b * s, d), 1, wup1, wdown1, tf=TILES.dff)
    xp = _ple(xp, pp, wgate1, wproj, layer=1, tm=TILES.rows_wide)
    (xs,) = ffn(xs3.reshape(bs * ts, d), 1, wup1, wdown1, tf=TILES.dff_ple, p=ps, wgate=wgate1)

    n_heads_kv = (n_kv, HEAD_DIM)
    return (
        xp.reshape(b, s, d),
        xs.reshape(bs, ts, d),
        k_p3[:, s - WINDOW:].reshape(1, b, WINDOW, *n_heads_kv),
        v_p3[:, s - WINDOW:].reshape(1, b, WINDOW, *n_heads_kv),
        tail_p[:, -1, HALO - POOL_HIST:][None],
        k_s3.reshape(1, bs, ts, *n_heads_kv),
        v_s3.reshape(1, bs, ts, *n_heads_kv),
        tail_s[:, HALO - POOL_HIST:][None],
    )
```

```python
import functools
import math
from typing import NamedTuple

import jax
import jax.numpy as jnp
from jax import lax
from jax.experimental import pallas as pl
from jax.experimental.pallas import tpu as pltpu

HEAD_DIM = 64
GQA_GROUP = 8
CHUNK = 64
WINDOW = 128
ROT_DIM = HEAD_DIM // 4
ROPE_THETA = 500000.0
POOL_WINDOWS = (2, 4, 8, 16)
POOL_HIST = max(POOL_WINDOWS) - 1
PAST_LEN = 2048
EPS = 1e-6
LOG2E = math.log2(math.e)

LANES = 128
BF16_ROWS = 16
HALO = 16
SUB_ROWS = 256
VMEM_LIMIT_BYTES = 60 * 1024 * 1024


class _Tiles(NamedTuple):
    rows: int = 512
    rows_wide: int = 1024
    attn_rows: int = 512
    dff: int = 2048
    dff_ple: int = 1024


TILES = _Tiles()

F32 = jnp.float32
BF16 = jnp.bfloat16


def _rms(x, g):
    ms = jnp.mean(x * x, axis=-1, keepdims=True)
    return x * lax.rsqrt(ms + EPS) * g


def _resident(shape, layer=None):
    if layer is None:
        nd = len(shape)
        return pl.BlockSpec(shape, lambda *_: (0,) * nd, pipeline_mode=pl.Buffered(1))
    nd = len(shape) - 1
    return pl.BlockSpec((None,) + tuple(shape[1:]), lambda *_: (layer,) + (0,) * nd, pipeline_mode=pl.Buffered(1))


def _params(*sem):
    return pltpu.CompilerParams(dimension_semantics=sem, vmem_limit_bytes=VMEM_LIMIT_BYTES)


def _cast_specs(casts, n_steps, flat_step):
    in_specs, out_specs, out_shapes, operands = [], [], [], []
    for w, layer in casts:
        _, r, c = w.shape
        rb = max(BF16_ROWS, r // n_steps)
        csplit = rb * n_steps // r
        assert r % rb == 0 and (r // rb) * csplit == n_steps and c % (csplit * LANES) == 0
        cw = c // csplit

        def in_map(*idx, layer=layer, csplit=csplit):
            k = flat_step(*idx)
            return (layer, k // csplit, k % csplit)

        def out_map(*idx, csplit=csplit):
            k = flat_step(*idx)
            return (k // csplit, k % csplit)

        in_specs.append(pl.BlockSpec((None, rb, cw), in_map))
        out_specs.append(pl.BlockSpec((rb, cw), out_map))
        out_shapes.append(jax.ShapeDtypeStruct((r, c), BF16))
        operands.append(w)
    return in_specs, out_specs, out_shapes, operands


def _run_casts(srcs, dsts):
    for src, dst in zip(srcs, dsts, strict=True):
        dst[...] = src[...].astype(BF16)


def _rope_tables(positions):
    half = ROT_DIM // 2
    inv = ROPE_THETA ** (-jnp.arange(0, ROT_DIM, 2, dtype=F32) / ROT_DIM)
    ang = positions.astype(F32)[:, None] * inv[None, :]
    cos, sin = jnp.cos(ang), jnp.sin(ang)
    p = positions.shape[0]
    ones = jnp.ones((p, HEAD_DIM - ROT_DIM), F32)
    zeros = jnp.zeros((p, HEAD_DIM - ROT_DIM), F32)
    zh = jnp.zeros((p, half), F32)
    c = jnp.concatenate([cos, cos, ones], axis=1)
    sa = jnp.concatenate([-sin, zh, zeros], axis=1)
    sb = jnp.concatenate([zh, sin, zeros], axis=1)
    rep = LANES // HEAD_DIM
    return tuple(jnp.tile(t, (1, rep)) for t in (c, sa, sb))


def _store_pair_layouts(blk, a_ref, b_ref, rows, col):
    lo = lax.broadcasted_iota(jnp.int32, blk.shape, 1) < HEAD_DIM
    swapped = pltpu.roll(blk, HEAD_DIM, 1)
    zero = jnp.zeros_like(blk)
    a_ref[rows, col:col + LANES] = jnp.where(lo, blk, zero).astype(BF16)
    b_ref[rows, col:col + LANES] = jnp.where(lo, zero, swapped).astype(BF16)
    a_ref[rows, col + LANES:col + 2 * LANES] = jnp.where(lo, swapped, zero).astype(BF16)
    b_ref[rows, col + LANES:col + 2 * LANES] = jnp.where(lo, zero, blk).astype(BF16)


N_QKV_IN = 7
N_QKV_OUT = 7


def _qkv_kernel(*refs, q_dim, kv_dim, nchunk):
    x_ref, g_ref, w_ref, b_ref, c_ref, sa_ref, sb_ref = refs[:N_QKV_IN]
    n_cast = (len(refs) - N_QKV_IN - N_QKV_OUT) // 2
    cast_in = refs[N_QKV_IN:N_QKV_IN + n_cast]
    q_ref, k_ref, v_ref, ka_ref, kb_ref, va_ref, vb_ref = refs[N_QKV_IN + n_cast:N_QKV_IN + n_cast + N_QKV_OUT]
    cast_out = refs[N_QKV_IN + n_cast + N_QKV_OUT:]
    _run_casts(cast_in, cast_out)

    scale = LOG2E / math.sqrt(HEAD_DIM)
    half = ROT_DIM // 2
    heads_per_blk = LANES // HEAD_DIM
    rot_cols = q_dim + kv_dim
    total = q_dim + 2 * kv_dim
    for r0 in range(0, x_ref.shape[0], SUB_ROWS):
        rows = slice(r0, r0 + SUB_ROWS)
        h = _rms(x_ref[rows, :], g_ref[...]).astype(BF16)
        c, sa, sb = c_ref[rows, :], sa_ref[rows, :], sb_ref[rows, :]
        cq, saq, sbq = c * scale, sa * scale, sb * scale
        for n0 in range(0, total, nchunk):
            acc = jnp.dot(h, w_ref[:, n0:n0 + nchunk], preferred_element_type=F32) + b_ref[:, n0:n0 + nchunk]
            for j0 in range(0, nchunk, LANES):
                col = n0 + j0
                blk = acc[:, j0:j0 + LANES]
                if col < q_dim:
                    blk = blk * cq + pltpu.roll(blk, LANES - half, 1) * saq + pltpu.roll(blk, half, 1) * sbq
                    q_ref[rows, col:col + LANES] = blk.astype(BF16)
                elif col < rot_cols:
                    blk = blk * c + pltpu.roll(blk, LANES - half, 1) * sa + pltpu.roll(blk, half, 1) * sb
                    k_ref[rows, col - q_dim:col - q_dim + LANES] = blk
                    _store_pair_layouts(blk, ka_ref, kb_ref, rows, (col - q_dim) * heads_per_blk)
                else:
                    v_ref[rows, col - rot_cols:col - rot_cols + LANES] = blk
                    _store_pair_layouts(blk, va_ref, vb_ref, rows, (col - rot_cols) * heads_per_blk)


def _qkv_rope(x, g, w_bf16, b, tables, *, tm, q_dim, kv_dim, casts=()):
    m, d = x.shape
    total = q_dim + 2 * kv_dim
    ptiles = tables[0].shape[0] // tm
    tab_spec = pl.BlockSpec((tm, LANES), lambda i: (i % ptiles, 0))
    kern = functools.partial(_qkv_kernel, q_dim=q_dim, kv_dim=kv_dim, nchunk=256)
    pair_w = kv_dim * (LANES // HEAD_DIM)
    c_in, c_out, c_shapes, c_ops = _cast_specs(casts, m // tm, lambda i: i)

    def rows(width):
        return pl.BlockSpec((tm, width), lambda i: (i, 0))

    return pl.pallas_call(
        kern,
        grid=(m // tm,),
        in_specs=[
            rows(d),
            _resident((1, d)),
            _resident((d, total)),
            _resident((1, total)),
            tab_spec, tab_spec, tab_spec,
        ] + c_in,
        out_specs=[rows(q_dim), rows(kv_dim), rows(kv_dim), rows(pair_w), rows(pair_w), rows(pair_w), rows(pair_w)]
        + c_out,
        out_shape=[
            jax.ShapeDtypeStruct((m, q_dim), BF16),
            jax.ShapeDtypeStruct((m, kv_dim), F32),
            jax.ShapeDtypeStruct((m, kv_dim), F32),
        ] + [jax.ShapeDtypeStruct((m, pair_w), BF16)] * 4 + c_shapes,
        compiler_params=_params("parallel"),
        name="qkv_rope",
    )(x, g, w_bf16, b, *tables, *c_ops)


def _chunks_per_body(chunks):
    return next(g for g in (4, 2, 1) if chunks % g == 0)


N_ATTN_IN = 6


def _attn_kernel(*refs, n_kv, chunks, group, hist_chunks):
    sink_ref, q_ref, ka_ref, kb_ref, va_ref, vb_ref = refs[:N_ATTN_IN]
    n_cast = (len(refs) - N_ATTN_IN - 2) // 2
    o_ref = refs[N_ATTN_IN + n_cast]
    s_scr = refs[-1]
    _run_casts(refs[N_ATTN_IN:N_ATTN_IN + n_cast], refs[N_ATTN_IN + n_cast + 1:N_ATTN_IN + 2 * n_cast + 1])
    step = pl.program_id(1)
    span = WINDOW + CHUNK
    pairs = GQA_GROUP // 2
    win_chunks = WINDOW // CHUNK
    neg_inf = -jnp.inf
    row_k = lax.broadcasted_iota(jnp.int32, (2 * span, LANES), 0)
    lane_k = lax.broadcasted_iota(jnp.int32, (2 * span, LANES), 1)
    ones_ab = jnp.where((row_k < span) == (lane_k < HEAD_DIM), 1.0, 0.0).astype(BF16)
    lo = lax.broadcasted_iota(jnp.int32, (CHUNK, LANES), 1) < HEAD_DIM
    nt = (((1,), (1,)), ((), ()))

    def body(j, carry, *, masked):
        units = []
        for u in range(group):
            ci = j * group + u
            ckv = step * chunks + ci + hist_chunks
            k0 = pl.multiple_of(jnp.maximum(ckv - win_chunks, 0) * CHUNK, CHUNK)
            units.append((pl.ds(pl.multiple_of(ci * CHUNK, CHUNK), CHUNK), pl.ds(k0, span), ckv))

        for u, (rows, win, _) in enumerate(units):
            for kh in range(n_kv):
                kcols = slice(kh * LANES, (kh + 1) * LANES)
                q_pairs = jnp.concatenate(
                    [q_ref[rows, (kh * pairs + p) * LANES:(kh * pairs + p + 1) * LANES] for p in range(pairs)], axis=0)
                k_ab = jnp.concatenate([ka_ref[win, kcols], kb_ref[win, kcols]], axis=0)
                s_scr[u, kh] = lax.dot_general(q_pairs, k_ab, nt, preferred_element_type=F32)

        e_all, t_all = [], []
        for u, (_, _, ckv) in enumerate(units):
            if masked:
                key_row = lax.broadcasted_iota(jnp.int32, (1, span), 1)
                bias1 = jnp.where(key_row < (ckv + 1) * CHUNK, 0.0, neg_inf)
                bias = jnp.concatenate([bias1, bias1], axis=1)
            for kh in range(n_kv):
                e_rows, t_rows = [], []
                for p in range(pairs):
                    s = s_scr[u, kh, p * CHUNK:(p + 1) * CHUNK, :]
                    if masked:
                        s = s + bias
                    t0, t1, t2 = s[:, :LANES], s[:, LANES:2 * LANES], s[:, 2 * LANES:]
                    sk_a = sink_ref[kh * GQA_GROUP + 2 * p] * LOG2E
                    sk_b = sink_ref[kh * GQA_GROUP + 2 * p + 1] * LOG2E
                    m_a = jnp.max(jnp.maximum(t0, jnp.where(lo, t1, neg_inf)), axis=-1, keepdims=True)
                    m_b = jnp.max(jnp.maximum(t2, jnp.where(lo, neg_inf, t1)), axis=-1, keepdims=True)
                    m_a, m_b = jnp.maximum(m_a, sk_a), jnp.maximum(m_b, sk_b)
                    m_ab = jnp.where(lo, m_a, m_b)
                    e = jnp.concatenate([jnp.exp2(t0 - m_a), jnp.exp2(t1 - m_ab), jnp.exp2(t2 - m_b)], axis=1)
                    e_rows.append(e.astype(BF16))
                    t_rows.append(jnp.exp2(jnp.where(lo[:1], sk_a, sk_b) - m_ab))
                e_all.append(jnp.concatenate(e_rows, axis=0))
                t_all.append(jnp.concatenate(t_rows, axis=0))

        for u, (rows, win, _) in enumerate(units):
            for kh in range(n_kv):
                kcols = slice(kh * LANES, (kh + 1) * LANES)
                v_ab = jnp.concatenate([va_ref[win, kcols], vb_ref[win, kcols]], axis=0)
                rhs = jnp.concatenate([v_ab, ones_ab], axis=1)
                acc = jnp.dot(e_all[u * n_kv + kh], rhs, preferred_element_type=F32)
                o_pairs = (acc[:, :LANES] / (acc[:, LANES:] + t_all[u * n_kv + kh])).astype(BF16)
                for p in range(pairs):
                    col = (kh * pairs + p) * LANES
                    o_ref[rows, col:col + LANES] = o_pairs[p * CHUNK:(p + 1) * CHUNK, :]
        return carry

    def run(masked):
        lax.fori_loop(0, chunks // group, functools.partial(body, masked=masked), 0)

    if hist_chunks >= win_chunks:
        run(False)
    else:
        needs_mask = step * chunks + hist_chunks < win_chunks
        pl.when(needs_mask)(lambda: run(True))
        pl.when(jnp.logical_not(needs_mask))(lambda: run(False))


def _attention(q, ka, kb, va, vb, sinks, *, tq, casts=()):
    b, s, q_dim = q.shape
    rows, kvw = ka.shape[1:]
    n_kv = kvw // LANES
    span = WINDOW + CHUNK
    chunks = tq // CHUNK
    assert rows >= span and (rows - s) % CHUNK == 0
    hist_spec = pl.BlockSpec((None, rows, kvw), lambda i, c: (i, 0, 0))
    group = _chunks_per_body(chunks)
    kern = functools.partial(_attn_kernel, n_kv=n_kv, chunks=chunks, group=group, hist_chunks=(rows - s) // CHUNK)
    steps = s // tq
    c_in, c_out, c_shapes, c_ops = _cast_specs(casts, b * steps, lambda i, c: i * steps + c)
    return pl.pallas_call(
        kern,
        grid=(b, steps),
        in_specs=[
            pl.BlockSpec(memory_space=pltpu.SMEM),
            pl.BlockSpec((None, tq, q_dim), lambda i, c: (i, c, 0)),
            hist_spec, hist_spec, hist_spec, hist_spec,
        ] + c_in,
        out_specs=[pl.BlockSpec((None, tq, q_dim), lambda i, c: (i, c, 0))] + c_out,
        out_shape=[jax.ShapeDtypeStruct((b, s, q_dim), BF16)] + c_shapes,
        scratch_shapes=[pltpu.VMEM((group, n_kv, (GQA_GROUP // 2) * CHUNK, 2 * span), F32)],
        compiler_params=_params("parallel", "arbitrary"),
        name="swa_attention",
    )(sinks, q, ka, kb, va, vb, *c_ops)


def _pair_layouts(x3, n_kv):
    b, r, _ = x3.shape
    x4 = x3.astype(BF16).reshape(b, r, n_kv, HEAD_DIM)
    lo = jnp.pad(x4, ((0, 0), (0, 0), (0, 0), (0, LANES - HEAD_DIM)))
    hi = jnp.pad(x4, ((0, 0), (0, 0), (0, 0), (LANES - HEAD_DIM, 0)))
    return lo.reshape(b, r, n_kv * LANES), hi.reshape(b, r, n_kv * LANES)


def _oproj_kernel(o_ref, x_ref, w_ref, g_ref, y_ref):
    mix = jnp.dot(o_ref[...], w_ref[...], preferred_element_type=F32)
    y_ref[...] = x_ref[...] + _rms(mix, g_ref[...])


def _out_proj(o, x, w_bf16, g, *, tm):
    m, d = x.shape
    return pl.pallas_call(
        _oproj_kernel,
        grid=(m // tm,),
        in_specs=[
            pl.BlockSpec((tm, o.shape[1]), lambda i: (i, 0)),
            pl.BlockSpec((tm, d), lambda i: (i, 0)),
            _resident(w_bf16.shape),
            _resident((1, d)),
        ],
        out_specs=pl.BlockSpec((tm, d), lambda i: (i, 0)),
        out_shape=jax.ShapeDtypeStruct((m, d), F32),
        compiler_params=_params("parallel"),
        name="attn_out_proj",
    )(o, x, w_bf16, g)


def _pool_group(h, prev, pos, w_ref, gi):
    w = POOL_WINDOWS[gi]
    group_dim = h.shape[1] // len(POOL_WINDOWS)
    sl = slice(gi * group_dim, (gi + 1) * group_dim)
    hg = h[:, sl]
    acc = jnp.concatenate([prev[:, sl], hg], axis=0)
    shift = 1
    while shift < w:
        acc = acc + pltpu.roll(acc, shift, 0)
        shift *= 2
    cnt = jnp.minimum(pos + 1, w).astype(F32)
    dgrp = (acc[HALO:, :] / cnt - hg).astype(BF16)
    return jnp.dot(dgrp, w_ref[gi], preferred_element_type=F32)


def _pool_mix(h, prev, pos, w_ref, scale):
    outs = [_pool_group(h, prev, pos, w_ref, gi) for gi in range(len(POOL_WINDOWS))]
    return jnp.concatenate(outs, axis=1) * scale


def _pool_kernel(x_ref, hist_ref, gpre_ref, gpost_ref, w_ref, scale_ref, y_ref, tail_ref, *, pos0):
    x = x_ref[...]
    rows = x.shape[0]
    h = _rms(x, gpre_ref[...])
    tail_ref[...] = h[rows - HALO:, :]
    pos = pos0 + lax.broadcasted_iota(jnp.int32, (rows, 1), 0)
    mix = _pool_mix(h, hist_ref[...], pos, w_ref, scale_ref[...])
    y_ref[...] = x + _rms(mix, gpost_ref[...])


def _pool_mixer(x, hist, gpre, gpost, w_bf16, scale, *, pos0):
    b, t, d = x.shape
    return pl.pallas_call(
        functools.partial(_pool_kernel, pos0=pos0),
        grid=(b,),
        in_specs=[
            pl.BlockSpec((None, t, d), lambda bi: (bi, 0, 0)),
            pl.BlockSpec((None, HALO, d), lambda bi: (bi, 0, 0)),
            _resident((1, d)),
            _resident((1, d)),
            _resident(w_bf16.shape),
            _resident((1, d)),
        ],
        out_specs=[
            pl.BlockSpec((None, t, d), lambda bi: (bi, 0, 0)),
            pl.BlockSpec((None, HALO, d), lambda bi: (bi, 0, 0)),
        ],
        out_shape=[
            jax.ShapeDtypeStruct((b, t, d), F32),
            jax.ShapeDtypeStruct((b, HALO, d), F32),
        ],
        compiler_params=_params("parallel"),
        name="pool_mixer",
    )(x, hist, gpre, gpost, w_bf16, scale)


def _ple_pool_kernel(x_ref, p_ref, wgate_ref, wproj_ref, gpre_ref, gpost_ref, w_ref, scale_ref,
                     y_ref, tail_ref, even_scr, odd_scr, halo_scr, *, tm, tiles_per_seq, n_tiles):
    k = pl.program_id(0)

    @pl.when(k == 0)
    def _():
        odd_scr[...] = jnp.zeros_like(odd_scr)
        halo_scr[...] = jnp.zeros_like(halo_scr)

    def step(dst_scr, src_scr):
        n_groups = len(POOL_WINDOWS)
        n_chunks = 2 * n_groups
        if dst_scr is not None:
            x = x_ref[...]
            xb, pb = x.astype(BF16), p_ref[...].astype(BF16)
            cw = x.shape[1] // n_chunks

        def embed_chunk(c):
            if dst_scr is None:
                return
            cols = slice(c * cw, (c + 1) * cw)
            gate = jax.nn.sigmoid(jnp.dot(xb, wgate_ref[:, cols], preferred_element_type=F32))
            emb = jnp.dot(pb, wproj_ref[:, cols], preferred_element_type=F32)
            dst_scr[:, cols] = x[:, cols] + gate * emb

        embed_chunk(0)
        seq_tile = lax.rem(jnp.maximum(k - 1, 0), tiles_per_seq)
        x1 = src_scr[...]
        h = _rms(x1, gpre_ref[...])
        prev = jnp.where(seq_tile == 0, 0.0, halo_scr[...])
        pos = seq_tile * tm + lax.broadcasted_iota(jnp.int32, (tm, 1), 0)
        tail = h[tm - HALO:, :]
        halo_scr[...] = tail
        tail_ref[...] = tail
        outs = []
        for gi in range(n_groups):
            outs.append(_pool_group(h, prev, pos, w_ref, gi))
            embed_chunk(gi + 1)
        mix = jnp.concatenate(outs, axis=1) * scale_ref[...]
        y_ref[...] = x1 + _rms(mix, gpost_ref[...])
        for c in range(n_groups + 1, n_chunks):
            embed_chunk(c)

    is_even = lax.rem(k, 2) == 0
    is_last = k == n_tiles
    last_src = even_scr if (n_tiles - 1) % 2 == 0 else odd_scr
    pl.when(jnp.logical_and(is_even, jnp.logical_not(is_last)))(lambda: step(even_scr, odd_scr))
    pl.when(jnp.logical_and(jnp.logical_not(is_even), jnp.logical_not(is_last)))(lambda: step(odd_scr, even_scr))
    pl.when(is_last)(lambda: step(None, last_src))


def _ple_pool(x, p, wgate, wproj, gpre, gpost, w_bf16, scale, *, layer, tm):
    b, s, d = x.shape
    nt = s // tm
    n_tiles = b * nt

    def cur(k):
        return jnp.minimum(k, n_tiles - 1)

    def lagged(k):
        return jnp.maximum(k - 1, 0)

    return pl.pallas_call(
        functools.partial(_ple_pool_kernel, tm=tm, tiles_per_seq=nt, n_tiles=n_tiles),
        grid=(n_tiles + 1,),
        in_specs=[
            pl.BlockSpec((None, tm, d), lambda k: (cur(k) // nt, cur(k) % nt, 0)),
            pl.BlockSpec((None, tm, p.shape[2]), lambda k: (layer, cur(k), 0)),
            _resident(wgate.shape),
            _resident(wproj.shape, layer),
            _resident((1, d)),
            _resident((1, d)),
            _resident(w_bf16.shape),
            _resident((1, d)),
        ],
        out_specs=[
            pl.BlockSpec((None, tm, d), lambda k: (lagged(k) // nt, lagged(k) % nt, 0)),
            pl.BlockSpec((None, None, HALO, d), lambda k: (lagged(k) // nt, lagged(k) % nt, 0, 0)),
        ],
        out_shape=[
            jax.ShapeDtypeStruct((b, s, d), F32),
            jax.ShapeDtypeStruct((b, nt, HALO, d), F32),
        ],
        scratch_shapes=[pltpu.VMEM((tm, d), F32), pltpu.VMEM((tm, d), F32), pltpu.VMEM((HALO, d), F32)],
        compiler_params=_params("arbitrary"),
        name="ple_pool",
    )(x, p, wgate, wproj, gpre, gpost, w_bf16, scale)


def _ple_kernel(x_ref, p_ref, wgate_ref, wproj_ref, y_ref):
    for r0 in range(0, x_ref.shape[0], SUB_ROWS):
        rows = slice(r0, r0 + SUB_ROWS)
        x = x_ref[rows, :]
        gate = jax.nn.sigmoid(jnp.dot(x.astype(BF16), wgate_ref[...], preferred_element_type=F32))
        emb = jnp.dot(p_ref[rows, :].astype(BF16), wproj_ref[...], preferred_element_type=F32)
        y_ref[rows, :] = x + gate * emb


def _ple(x, p, wgate, wproj, *, layer, tm):
    m, d = x.shape
    return pl.pallas_call(
        _ple_kernel,
        grid=(m // tm,),
        in_specs=[
            pl.BlockSpec((tm, d), lambda i: (i, 0)),
            pl.BlockSpec((None, tm, p.shape[2]), lambda i: (layer, i, 0)),
            _resident(wgate.shape),
            _resident(wproj.shape, layer),
        ],
        out_specs=pl.BlockSpec((tm, d), lambda i: (i, 0)),
        out_shape=jax.ShapeDtypeStruct((m, d), F32),
        compiler_params=_params("parallel"),
        name="ple",
    )(x, p, wgate, wproj)


N_MLP_IN = 5
N_PLE_IN = 3


def _ffn_kernel(*refs, nf, with_ple):
    x_ref, gpre_ref, gpost_ref, wup_ref, wdown_ref = refs[:N_MLP_IN]
    n_in = N_MLP_IN + (N_PLE_IN if with_ple else 0)
    if with_ple:
        p_ref, wgate_ref, wproj_ref = refs[N_MLP_IN:n_in]
    n_cast = (len(refs) - n_in - 2) // 2
    cast_in = refs[n_in:n_in + n_cast]
    y_ref = refs[n_in + n_cast]
    cast_out = refs[n_in + n_cast + 1:n_in + 2 * n_cast + 1]
    h_scr = refs[-1]
    _run_casts(cast_in, cast_out)
    f = pl.program_id(1)

    def mlp_part(h):
        u = jnp.maximum(jnp.dot(h, wup_ref[...], preferred_element_type=F32), 0.0)
        return jnp.dot((u * u).astype(BF16), wdown_ref[...], preferred_element_type=F32)

    @pl.when(f == 0)
    def _():
        h = _rms(x_ref[...], gpre_ref[...]).astype(BF16)
        h_scr[...] = h
        y_ref[...] = mlp_part(h)

    @pl.when(jnp.logical_and(f > 0, f < nf - 1))
    def _():
        y_ref[...] += mlp_part(h_scr[...])

    @pl.when(f == nf - 1)
    def _():
        x1 = x_ref[...] + _rms(y_ref[...] + mlp_part(h_scr[...]), gpost_ref[...])
        if with_ple:
            emb = jnp.dot(p_ref[...].astype(BF16), wproj_ref[...], preferred_element_type=F32)
            gate = jax.nn.sigmoid(jnp.dot(x1.astype(BF16), wgate_ref[...], preferred_element_type=F32))
            x1 = x1 + gate * emb
        y_ref[...] = x1


def _ffn(x, gpre, gpost, wup, wdown, *, tm, tf, ple=None, casts=()):
    m, d = x.shape
    dff = wup.shape[1]
    nf = dff // tf
    assert nf >= 2
    c_in, c_out, c_shapes, c_ops = _cast_specs(casts, (m // tm) * nf, lambda i, f: i * nf + f)
    ple_specs, ple_ops = [], []
    if ple is not None:
        p, wgate, wproj, layer = ple
        ple_specs = [
            pl.BlockSpec((None, tm, p.shape[2]), lambda i, f: (layer, i, 0)),
            _resident(wgate.shape),
            _resident(wproj.shape, layer),
        ]
        ple_ops = [p, wgate, wproj]
    return pl.pallas_call(
        functools.partial(_ffn_kernel, nf=nf, with_ple=ple is not None),
        grid=(m // tm, nf),
        in_specs=[
            pl.BlockSpec((tm, d), lambda i, f: (i, 0)),
            _resident((1, d)),
            _resident((1, d)),
            pl.BlockSpec((d, tf), lambda i, f: (0, f)),
            pl.BlockSpec((tf, d), lambda i, f: (f, 0)),
        ] + ple_specs + c_in,
        out_specs=[pl.BlockSpec((tm, d), lambda i, f: (i, 0))] + c_out,
        out_shape=[jax.ShapeDtypeStruct((m, d), F32)] + c_shapes,
        scratch_shapes=[pltpu.VMEM((tm, d), BF16)],
        compiler_params=_params("parallel", "arbitrary"),
        name="ffn_ple" if ple is not None else "ffn",
    )(x, gpre, gpost, wup, wdown, *ple_ops, *c_ops)


def _row(v):
    return v.reshape(1, -1)


def kernel(x_prompt, x_sample, cache_k, cache_v, state_pool, p_prompt, p_sample, norm_mix_pre, norm_mix_post, norm_ffn_pre, norm_ffn_post, w_qkv, b_qkv, w_o, sinks, w_pool, pool_scale, w_ffn_up, w_ffn_down, w_ple_proj, w_ple_gate):
    b, s, d = x_prompt.shape
    bs, ts, _ = x_sample.shape
    q_dim = w_o.shape[1]
    kv_dim = (w_qkv.shape[2] - q_dim) // 2
    n_kv = kv_dim // HEAD_DIM
    tm = TILES.rows
    assert s % tm == 0 and (bs * ts) % tm == 0 and tm % ts == 0

    def bf(w):
        return w.astype(BF16)

    xp = x_prompt.reshape(b * s, d)
    xs = x_sample.reshape(bs * ts, d)
    wproj = bf(w_ple_proj)
    pp = p_prompt.reshape(p_prompt.shape[0], b * s, -1)
    ps = p_sample.reshape(p_sample.shape[0], bs * ts, -1)

    def ffn(x, i, wup, wdown, *, tf, p=None, wgate=None, casts=()):
        ple = None if p is None else (p, wgate, wproj, i)
        return _ffn(x, _row(norm_ffn_pre[i]), _row(norm_ffn_post[i]), wup, wdown, tm=tm, tf=tf, ple=ple, casts=casts)

    tab_p = _rope_tables(jnp.arange(s, dtype=jnp.int32))
    tab_s = tuple(jnp.tile(t, (tm // ts, 1)) for t in _rope_tables(PAST_LEN + jnp.arange(ts, dtype=jnp.int32)))
    wqkv, bqkv = bf(w_qkv[0]), _row(b_qkv[0])
    g_pre, g_post = _row(norm_mix_pre[0]), _row(norm_mix_post[0])

    q_p, k_p, v_p, *kv_p = _qkv_rope(xp, g_pre, wqkv, bqkv, tab_p, tm=TILES.rows_wide, q_dim=q_dim, kv_dim=kv_dim)
    q_s, k_s, v_s, *kv_s = _qkv_rope(xs, g_pre, wqkv, bqkv, tab_s, tm=tm, q_dim=q_dim, kv_dim=kv_dim)

    k_p3, v_p3 = k_p.reshape(b, s, kv_dim), v_p.reshape(b, s, kv_dim)
    k_s3, v_s3 = k_s.reshape(bs, ts, kv_dim), v_s.reshape(bs, ts, kv_dim)
    casts0 = ((w_ffn_up, 0), (w_ffn_down, 0), (w_ple_gate, 0), (w_ple_gate, 1), (w_o, 0))
    o_p, wup0, wdown0, wgate0, wgate1, wo = _attention(
        q_p.reshape(b, s, q_dim), *(t.reshape(b, s, -1) for t in kv_p), sinks[0], tq=TILES.attn_rows, casts=casts0)
    cache = (*_pair_layouts(cache_k[0].reshape(bs, WINDOW, kv_dim), n_kv),
             *_pair_layouts(cache_v[0].reshape(bs, WINDOW, kv_dim), n_kv))
    kv_s = (jnp.concatenate([c, t.reshape(bs, ts, -1)], axis=1) for c, t in zip(cache, kv_s))
    (o_s,) = _attention(q_s.reshape(bs, ts, q_dim), *kv_s, sinks[0], tq=ts)

    xp = _out_proj(o_p.reshape(b * s, q_dim), xp, wo, g_post, tm=TILES.rows_wide)
    xs = _out_proj(o_s.reshape(bs * ts, q_dim), xs, wo, g_post, tm=tm)

    xp, wup1, wdown1 = ffn(xp, 0, wup0, wdown0, tf=TILES.dff, casts=((w_ffn_up, 1), (w_ffn_down, 1)))
    (xs,) = ffn(xs, 0, wup0, wdown0, tf=TILES.dff_ple, p=ps, wgate=wgate0)

    g_pre, g_post = _row(norm_mix_pre[1]), _row(norm_mix_post[1])
    wpool, pscale = bf(w_pool[0]), _row(pool_scale[0])
    xp3, tail_p = _ple_pool(xp.reshape(b, s, d), pp, wgate0, wproj, g_pre, g_post, wpool, pscale,
                            layer=0, tm=tm)
    hist = jnp.pad(state_pool[0], ((0, 0), (HALO - POOL_HIST, 0), (0, 0)))
    xs3, tail_s = _pool_mixer(xs.reshape(bs, ts, d), hist, g_pre, g_post, wpool, pscale, pos0=PAST_LEN)

    (xp,) = ffn(xp3.reshape(b * s, d), 1, wup1, wdown1, tf=TILES.dff)
    xp = _ple(xp, pp, wgate1, wproj, layer=1, tm=TILES.rows_wide)
    (xs,) = ffn(xs3.reshape(bs * ts, d), 1, wup1, wdown1, tf=TILES.dff_ple, p=ps, wgate=wgate1)

    n_heads_kv = (n_kv, HEAD_DIM)
    return (
        xp.reshape(b, s, d),
        xs.reshape(bs, ts, d),
        k_p3[:, s - WINDOW:].reshape(1, b, WINDOW, *n_heads_kv),
        v_p3[:, s - WINDOW:].reshape(1, b, WINDOW, *n_heads_kv),
        tail_p[:, -1, HALO - POOL_HIST:][None],
        k_s3.reshape(1, bs, ts, *n_heads_kv),
        v_s3.reshape(1, bs, ts, *n_heads_kv),
        tail_s[:, HALO - POOL_HIST:][None],
    )
```

```python
import functools
import math
from typing import NamedTuple

import jax
import jax.numpy as jnp
from jax import lax
from jax.experimental import pallas as pl
from jax.experimental.pallas import tpu as pltpu

HEAD_DIM = 64
GQA_GROUP = 8
CHUNK = 64
WINDOW = 128
ROT_DIM = HEAD_DIM // 4
ROPE_THETA = 500000.0
POOL_WINDOWS = (2, 4, 8, 16)
POOL_HIST = max(POOL_WINDOWS) - 1
PAST_LEN = 2048
EPS = 1e-6
LOG2E = math.log2(math.e)

LANES = 128
BF16_ROWS = 16
HALO = 16
SUB_ROWS = 256
VMEM_LIMIT_BYTES = 60 * 1024 * 1024


class _Tiles(NamedTuple):
    rows: int = 512
    rows_wide: int = 1024
    attn_rows: int = 512
    dff: int = 2048
    dff_ple: int = 1024
    dff_f32: int = 512


TILES = _Tiles()

F32 = jnp.float32
BF16 = jnp.bfloat16


def _rms(x, g):
    ms = jnp.mean(x * x, axis=-1, keepdims=True)
    return x * lax.rsqrt(ms + EPS) * g


def _resident(shape, layer=None):
    if layer is None:
        nd = len(shape)
        return pl.BlockSpec(shape, lambda *_: (0,) * nd, pipeline_mode=pl.Buffered(1))
    nd = len(shape) - 1
    return pl.BlockSpec((None,) + tuple(shape[1:]), lambda *_: (layer,) + (0,) * nd, pipeline_mode=pl.Buffered(1))


def _params(*sem):
    return pltpu.CompilerParams(dimension_semantics=sem, vmem_limit_bytes=VMEM_LIMIT_BYTES)


def _cast_specs(casts, n_steps, flat_step):
    in_specs, out_specs, out_shapes, operands = [], [], [], []
    for w, layer in casts:
        _, r, c = w.shape
        rb = max(BF16_ROWS, r // n_steps)
        csplit = rb * n_steps // r
        assert r % rb == 0 and (r // rb) * csplit == n_steps and c % (csplit * LANES) == 0
        cw = c // csplit

        def in_map(*idx, layer=layer, csplit=csplit):
            k = flat_step(*idx)
            return (layer, k // csplit, k % csplit)

        def out_map(*idx, csplit=csplit):
            k = flat_step(*idx)
            return (k // csplit, k % csplit)

        in_specs.append(pl.BlockSpec((None, rb, cw), in_map))
        out_specs.append(pl.BlockSpec((rb, cw), out_map))
        out_shapes.append(jax.ShapeDtypeStruct((r, c), BF16))
        operands.append(w)
    return in_specs, out_specs, out_shapes, operands


def _run_casts(srcs, dsts):
    for src, dst in zip(srcs, dsts, strict=True):
        dst[...] = src[...].astype(BF16)


def _rope_tables(positions):
    half = ROT_DIM // 2
    inv = ROPE_THETA ** (-jnp.arange(0, ROT_DIM, 2, dtype=F32) / ROT_DIM)
    ang = positions.astype(F32)[:, None] * inv[None, :]
    cos, sin = jnp.cos(ang), jnp.sin(ang)
    p = positions.shape[0]
    ones = jnp.ones((p, HEAD_DIM - ROT_DIM), F32)
    zeros = jnp.zeros((p, HEAD_DIM - ROT_DIM), F32)
    zh = jnp.zeros((p, half), F32)
    c = jnp.concatenate([cos, cos, ones], axis=1)
    sa = jnp.concatenate([-sin, zh, zeros], axis=1)
    sb = jnp.concatenate([zh, sin, zeros], axis=1)
    rep = LANES // HEAD_DIM
    return tuple(jnp.tile(t, (1, rep)) for t in (c, sa, sb))


def _store_pair_layouts(blk, a_ref, b_ref, rows, col):
    lo = lax.broadcasted_iota(jnp.int32, blk.shape, 1) < HEAD_DIM
    swapped = pltpu.roll(blk, HEAD_DIM, 1)
    zero = jnp.zeros_like(blk)
    a_ref[rows, col:col + LANES] = jnp.where(lo, blk, zero).astype(BF16)
    b_ref[rows, col:col + LANES] = jnp.where(lo, zero, swapped).astype(BF16)
    a_ref[rows, col + LANES:col + 2 * LANES] = jnp.where(lo, swapped, zero).astype(BF16)
    b_ref[rows, col + LANES:col + 2 * LANES] = jnp.where(lo, zero, blk).astype(BF16)


N_QKV_IN = 7
N_QKV_OUT = 7


def _qkv_kernel(*refs, q_dim, kv_dim, nchunk):
    x_ref, g_ref, w_ref, b_ref, c_ref, sa_ref, sb_ref = refs[:N_QKV_IN]
    n_cast = (len(refs) - N_QKV_IN - N_QKV_OUT) // 2
    cast_in = refs[N_QKV_IN:N_QKV_IN + n_cast]
    q_ref, k_ref, v_ref, ka_ref, kb_ref, va_ref, vb_ref = refs[N_QKV_IN + n_cast:N_QKV_IN + n_cast + N_QKV_OUT]
    cast_out = refs[N_QKV_IN + n_cast + N_QKV_OUT:]
    _run_casts(cast_in, cast_out)

    scale = LOG2E / math.sqrt(HEAD_DIM)
    half = ROT_DIM // 2
    heads_per_blk = LANES // HEAD_DIM
    rot_cols = q_dim + kv_dim
    total = q_dim + 2 * kv_dim
    for r0 in range(0, x_ref.shape[0], SUB_ROWS):
        rows = slice(r0, r0 + SUB_ROWS)
        h = _rms(x_ref[rows, :], g_ref[...]).astype(BF16)
        c, sa, sb = c_ref[rows, :], sa_ref[rows, :], sb_ref[rows, :]
        cq, saq, sbq = c * scale, sa * scale, sb * scale
        for n0 in range(0, total, nchunk):
            acc = jnp.dot(h, w_ref[:, n0:n0 + nchunk], preferred_element_type=F32) + b_ref[:, n0:n0 + nchunk]
            for j0 in range(0, nchunk, LANES):
                col = n0 + j0
                blk = acc[:, j0:j0 + LANES]
                if col < q_dim:
                    blk = blk * cq + pltpu.roll(blk, LANES - half, 1) * saq + pltpu.roll(blk, half, 1) * sbq
                    q_ref[rows, col:col + LANES] = blk.astype(BF16)
                elif col < rot_cols:
                    blk = blk * c + pltpu.roll(blk, LANES - half, 1) * sa + pltpu.roll(blk, half, 1) * sb
                    k_ref[rows, col - q_dim:col - q_dim + LANES] = blk
                    _store_pair_layouts(blk, ka_ref, kb_ref, rows, (col - q_dim) * heads_per_blk)
                else:
                    v_ref[rows, col - rot_cols:col - rot_cols + LANES] = blk
                    _store_pair_layouts(blk, va_ref, vb_ref, rows, (col - rot_cols) * heads_per_blk)


def _qkv_rope(x, g, w_bf16, b, tables, *, tm, q_dim, kv_dim, casts=()):
    m, d = x.shape
    total = q_dim + 2 * kv_dim
    ptiles = tables[0].shape[0] // tm
    tab_spec = pl.BlockSpec((tm, LANES), lambda i: (i % ptiles, 0))
    kern = functools.partial(_qkv_kernel, q_dim=q_dim, kv_dim=kv_dim, nchunk=256)
    pair_w = kv_dim * (LANES // HEAD_DIM)
    c_in, c_out, c_shapes, c_ops = _cast_specs(casts, m // tm, lambda i: i)

    def rows(width):
        return pl.BlockSpec((tm, width), lambda i: (i, 0))

    return pl.pallas_call(
        kern,
        grid=(m // tm,),
        in_specs=[
            rows(d),
            _resident((1, d)),
            _resident((d, total)),
            _resident((1, total)),
            tab_spec, tab_spec, tab_spec,
        ] + c_in,
        out_specs=[rows(q_dim), rows(kv_dim), rows(kv_dim), rows(pair_w), rows(pair_w), rows(pair_w), rows(pair_w)]
        + c_out,
        out_shape=[
            jax.ShapeDtypeStruct((m, q_dim), BF16),
            jax.ShapeDtypeStruct((m, kv_dim), F32),
            jax.ShapeDtypeStruct((m, kv_dim), F32),
        ] + [jax.ShapeDtypeStruct((m, pair_w), BF16)] * 4 + c_shapes,
        compiler_params=_params("parallel"),
        name="qkv_rope",
    )(x, g, w_bf16, b, *tables, *c_ops)


def _chunks_per_body(chunks):
    return next(g for g in (4, 2, 1) if chunks % g == 0)


def _attn_kernel(sink_ref, q_ref, ka_ref, kb_ref, va_ref, vb_ref, o_ref, s_scr, *, n_kv, chunks, group, hist_chunks):
    step = pl.program_id(1)
    span = WINDOW + CHUNK
    pairs = GQA_GROUP // 2
    win_chunks = WINDOW // CHUNK
    neg_inf = -jnp.inf
    row_k = lax.broadcasted_iota(jnp.int32, (2 * span, LANES), 0)
    lane_k = lax.broadcasted_iota(jnp.int32, (2 * span, LANES), 1)
    ones_ab = jnp.where((row_k < span) == (lane_k < HEAD_DIM), 1.0, 0.0).astype(BF16)
    lo = lax.broadcasted_iota(jnp.int32, (CHUNK, LANES), 1) < HEAD_DIM
    nt = (((1,), (1,)), ((), ()))

    def body(j, carry, *, masked):
        units = []
        for u in range(group):
            ci = j * group + u
            ckv = step * chunks + ci + hist_chunks
            k0 = pl.multiple_of(jnp.maximum(ckv - win_chunks, 0) * CHUNK, CHUNK)
            units.append((pl.ds(pl.multiple_of(ci * CHUNK, CHUNK), CHUNK), pl.ds(k0, span), ckv))

        for u, (rows, win, _) in enumerate(units):
            for kh in range(n_kv):
                kcols = slice(kh * LANES, (kh + 1) * LANES)
                q_pairs = jnp.concatenate(
                    [q_ref[rows, (kh * pairs + p) * LANES:(kh * pairs + p + 1) * LANES] for p in range(pairs)], axis=0)
                k_ab = jnp.concatenate([ka_ref[win, kcols], kb_ref[win, kcols]], axis=0)
                s_scr[u, kh] = lax.dot_general(q_pairs, k_ab, nt, preferred_element_type=F32)

        e_all, t_all = [], []
        for u, (_, _, ckv) in enumerate(units):
            if masked:
                key_row = lax.broadcasted_iota(jnp.int32, (1, span), 1)
                bias1 = jnp.where(key_row < (ckv + 1) * CHUNK, 0.0, neg_inf)
                bias = jnp.concatenate([bias1, bias1], axis=1)
            for kh in range(n_kv):
                e_rows, t_rows = [], []
                for p in range(pairs):
                    s = s_scr[u, kh, p * CHUNK:(p + 1) * CHUNK, :]
                    if masked:
                        s = s + bias
                    t0, t1, t2 = s[:, :LANES], s[:, LANES:2 * LANES], s[:, 2 * LANES:]
                    sk_a = sink_ref[kh * GQA_GROUP + 2 * p] * LOG2E
                    sk_b = sink_ref[kh * GQA_GROUP + 2 * p + 1] * LOG2E
                    m_a = jnp.max(jnp.maximum(t0, jnp.where(lo, t1, neg_inf)), axis=-1, keepdims=True)
                    m_b = jnp.max(jnp.maximum(t2, jnp.where(lo, neg_inf, t1)), axis=-1, keepdims=True)
                    m_a, m_b = jnp.maximum(m_a, sk_a), jnp.maximum(m_b, sk_b)
                    m_ab = jnp.where(lo, m_a, m_b)
                    e = jnp.concatenate([jnp.exp2(t0 - m_a), jnp.exp2(t1 - m_ab), jnp.exp2(t2 - m_b)], axis=1)
                    e_rows.append(e.astype(BF16))
                    t_rows.append(jnp.exp2(jnp.where(lo[:1], sk_a, sk_b) - m_ab))
                e_all.append(jnp.concatenate(e_rows, axis=0))
                t_all.append(jnp.concatenate(t_rows, axis=0))

        for u, (rows, win, _) in enumerate(units):
            for kh in range(n_kv):
                kcols = slice(kh * LANES, (kh + 1) * LANES)
                v_ab = jnp.concatenate([va_ref[win, kcols], vb_ref[win, kcols]], axis=0)
                rhs = jnp.concatenate([v_ab, ones_ab], axis=1)
                acc = jnp.dot(e_all[u * n_kv + kh], rhs, preferred_element_type=F32)
                o_pairs = (acc[:, :LANES] / (acc[:, LANES:] + t_all[u * n_kv + kh])).astype(BF16)
                for p in range(pairs):
                    col = (kh * pairs + p) * LANES
                    o_ref[rows, col:col + LANES] = o_pairs[p * CHUNK:(p + 1) * CHUNK, :]
        return carry

    def run(masked):
        lax.fori_loop(0, chunks // group, functools.partial(body, masked=masked), 0)

    if hist_chunks >= win_chunks:
        run(False)
    else:
        needs_mask = step * chunks + hist_chunks < win_chunks
        pl.when(needs_mask)(lambda: run(True))
        pl.when(jnp.logical_not(needs_mask))(lambda: run(False))


def _attention(q, ka, kb, va, vb, sinks, *, tq):
    b, s, q_dim = q.shape
    rows, kvw = ka.shape[1:]
    n_kv = kvw // LANES
    span = WINDOW + CHUNK
    chunks = tq // CHUNK
    assert rows >= span and (rows - s) % CHUNK == 0
    hist_spec = pl.BlockSpec((None, rows, kvw), lambda i, c: (i, 0, 0))
    group = _chunks_per_body(chunks)
    kern = functools.partial(_attn_kernel, n_kv=n_kv, chunks=chunks, group=group, hist_chunks=(rows - s) // CHUNK)
    return pl.pallas_call(
        kern,
        grid=(b, s // tq),
        in_specs=[
            pl.BlockSpec(memory_space=pltpu.SMEM),
            pl.BlockSpec((None, tq, q_dim), lambda i, c: (i, c, 0)),
            hist_spec, hist_spec, hist_spec, hist_spec,
        ],
        out_specs=pl.BlockSpec((None, tq, q_dim), lambda i, c: (i, c, 0)),
        out_shape=jax.ShapeDtypeStruct((b, s, q_dim), BF16),
        scratch_shapes=[pltpu.VMEM((group, n_kv, (GQA_GROUP // 2) * CHUNK, 2 * span), F32)],
        compiler_params=_params("parallel", "arbitrary"),
        name="swa_attention",
    )(sinks, q, ka, kb, va, vb)


def _pair_layouts(x3, n_kv):
    b, r, _ = x3.shape
    x4 = x3.astype(BF16).reshape(b, r, n_kv, HEAD_DIM)
    lo = jnp.pad(x4, ((0, 0), (0, 0), (0, 0), (0, LANES - HEAD_DIM)))
    hi = jnp.pad(x4, ((0, 0), (0, 0), (0, 0), (LANES - HEAD_DIM, 0)))
    return lo.reshape(b, r, n_kv * LANES), hi.reshape(b, r, n_kv * LANES)


def _oproj_kernel(o_ref, x_ref, w_ref, g_ref, y_ref):
    mix = jnp.dot(o_ref[...], w_ref[...], preferred_element_type=F32)
    y_ref[...] = x_ref[...] + _rms(mix, g_ref[...])


def _out_proj(o, x, w_bf16, g, *, tm):
    m, d = x.shape
    return pl.pallas_call(
        _oproj_kernel,
        grid=(m // tm,),
        in_specs=[
            pl.BlockSpec((tm, o.shape[1]), lambda i: (i, 0)),
            pl.BlockSpec((tm, d), lambda i: (i, 0)),
            _resident(w_bf16.shape),
            _resident((1, d)),
        ],
        out_specs=pl.BlockSpec((tm, d), lambda i: (i, 0)),
        out_shape=jax.ShapeDtypeStruct((m, d), F32),
        compiler_params=_params("parallel"),
        name="attn_out_proj",
    )(o, x, w_bf16, g)


def _pool_group(h, prev, pos, w_ref, gi):
    w = POOL_WINDOWS[gi]
    group_dim = h.shape[1] // len(POOL_WINDOWS)
    sl = slice(gi * group_dim, (gi + 1) * group_dim)
    hg = h[:, sl]
    acc = jnp.concatenate([prev[:, sl], hg], axis=0)
    shift = 1
    while shift < w:
        acc = acc + pltpu.roll(acc, shift, 0)
        shift *= 2
    cnt = jnp.minimum(pos + 1, w).astype(F32)
    dgrp = (acc[HALO:, :] / cnt - hg).astype(BF16)
    return jnp.dot(dgrp, w_ref[gi], preferred_element_type=F32)


def _pool_mix(h, prev, pos, w_ref, scale):
    outs = [_pool_group(h, prev, pos, w_ref, gi) for gi in range(len(POOL_WINDOWS))]
    return jnp.concatenate(outs, axis=1) * scale


def _pool_kernel(x_ref, hist_ref, gpre_ref, gpost_ref, w_ref, scale_ref, y_ref, tail_ref, *, pos0):
    x = x_ref[...]
    rows = x.shape[0]
    h = _rms(x, gpre_ref[...])
    tail_ref[...] = h[rows - HALO:, :]
    pos = pos0 + lax.broadcasted_iota(jnp.int32, (rows, 1), 0)
    mix = _pool_mix(h, hist_ref[...], pos, w_ref, scale_ref[...])
    y_ref[...] = x + _rms(mix, gpost_ref[...])


def _pool_mixer(x, hist, gpre, gpost, w_bf16, scale, *, pos0):
    b, t, d = x.shape
    return pl.pallas_call(
        functools.partial(_pool_kernel, pos0=pos0),
        grid=(b,),
        in_specs=[
            pl.BlockSpec((None, t, d), lambda bi: (bi, 0, 0)),
            pl.BlockSpec((None, HALO, d), lambda bi: (bi, 0, 0)),
            _resident((1, d)),
            _resident((1, d)),
            _resident(w_bf16.shape),
            _resident((1, d)),
        ],
        out_specs=[
            pl.BlockSpec((None, t, d), lambda bi: (bi, 0, 0)),
            pl.BlockSpec((None, HALO, d), lambda bi: (bi, 0, 0)),
        ],
        out_shape=[
            jax.ShapeDtypeStruct((b, t, d), F32),
            jax.ShapeDtypeStruct((b, HALO, d), F32),
        ],
        compiler_params=_params("parallel"),
        name="pool_mixer",
    )(x, hist, gpre, gpost, w_bf16, scale)


def _ple_pool_kernel(x_ref, p_ref, wgate_ref, wproj_ref, gpre_ref, gpost_ref, w_ref, scale_ref,
                     y_ref, tail_ref, even_scr, odd_scr, halo_scr, *, tm, tiles_per_seq, n_tiles):
    k = pl.program_id(0)

    @pl.when(k == 0)
    def _():
        odd_scr[...] = jnp.zeros_like(odd_scr)
        halo_scr[...] = jnp.zeros_like(halo_scr)

    def step(dst_scr, src_scr):
        n_groups = len(POOL_WINDOWS)
        n_chunks = 2 * n_groups
        if dst_scr is not None:
            x = x_ref[...]
            xb, pb = x.astype(BF16), p_ref[...].astype(BF16)
            cw = x.shape[1] // n_chunks

        def embed_chunk(c):
            if dst_scr is None:
                return
            cols = slice(c * cw, (c + 1) * cw)
            gate = jax.nn.sigmoid(jnp.dot(xb, wgate_ref[:, cols], preferred_element_type=F32))
            emb = jnp.dot(pb, wproj_ref[:, cols], preferred_element_type=F32)
            dst_scr[:, cols] = x[:, cols] + gate * emb

        embed_chunk(0)
        seq_tile = lax.rem(jnp.maximum(k - 1, 0), tiles_per_seq)
        x1 = src_scr[...]
        h = _rms(x1, gpre_ref[...])
        prev = jnp.where(seq_tile == 0, 0.0, halo_scr[...])
        pos = seq_tile * tm + lax.broadcasted_iota(jnp.int32, (tm, 1), 0)
        tail = h[tm - HALO:, :]
        halo_scr[...] = tail
        tail_ref[...] = tail
        outs = []
        for gi in range(n_groups):
            outs.append(_pool_group(h, prev, pos, w_ref, gi))
            embed_chunk(gi + 1)
        mix = jnp.concatenate(outs, axis=1) * scale_ref[...]
        y_ref[...] = x1 + _rms(mix, gpost_ref[...])
        for c in range(n_groups + 1, n_chunks):
            embed_chunk(c)

    is_even = lax.rem(k, 2) == 0
    is_last = k == n_tiles
    last_src = even_scr if (n_tiles - 1) % 2 == 0 else odd_scr
    pl.when(jnp.logical_and(is_even, jnp.logical_not(is_last)))(lambda: step(even_scr, odd_scr))
    pl.when(jnp.logical_and(jnp.logical_not(is_even), jnp.logical_not(is_last)))(lambda: step(odd_scr, even_scr))
    pl.when(is_last)(lambda: step(None, last_src))


def _ple_pool(x, p, wgate, wproj, gpre, gpost, w_bf16, scale, *, layer, tm):
    b, s, d = x.shape
    nt = s // tm
    n_tiles = b * nt

    def cur(k):
        return jnp.minimum(k, n_tiles - 1)

    def lagged(k):
        return jnp.maximum(k - 1, 0)

    return pl.pallas_call(
        functools.partial(_ple_pool_kernel, tm=tm, tiles_per_seq=nt, n_tiles=n_tiles),
        grid=(n_tiles + 1,),
        in_specs=[
            pl.BlockSpec((None, tm, d), lambda k: (cur(k) // nt, cur(k) % nt, 0)),
            pl.BlockSpec((None, tm, p.shape[2]), lambda k: (layer, cur(k), 0)),
            _resident(wgate.shape),
            _resident(wproj.shape, layer),
            _resident((1, d)),
            _resident((1, d)),
            _resident(w_bf16.shape),
            _resident((1, d)),
        ],
        out_specs=[
            pl.BlockSpec((None, tm, d), lambda k: (lagged(k) // nt, lagged(k) % nt, 0)),
            pl.BlockSpec((None, None, HALO, d), lambda k: (lagged(k) // nt, lagged(k) % nt, 0, 0)),
        ],
        out_shape=[
            jax.ShapeDtypeStruct((b, s, d), F32),
            jax.ShapeDtypeStruct((b, nt, HALO, d), F32),
        ],
        scratch_shapes=[pltpu.VMEM((tm, d), F32), pltpu.VMEM((tm, d), F32), pltpu.VMEM((HALO, d), F32)],
        compiler_params=_params("arbitrary"),
        name="ple_pool",
    )(x, p, wgate, wproj, gpre, gpost, w_bf16, scale)


def _ple_kernel(x_ref, p_ref, wgate_ref, wproj_ref, y_ref):
    for r0 in range(0, x_ref.shape[0], SUB_ROWS):
        rows = slice(r0, r0 + SUB_ROWS)
        x = x_ref[rows, :]
        gate = jax.nn.sigmoid(jnp.dot(x.astype(BF16), wgate_ref[...], preferred_element_type=F32))
        emb = jnp.dot(p_ref[rows, :].astype(BF16), wproj_ref[...], preferred_element_type=F32)
        y_ref[rows, :] = x + gate * emb


def _ple(x, p, wgate, wproj, *, layer, tm):
    m, d = x.shape
    return pl.pallas_call(
        _ple_kernel,
        grid=(m // tm,),
        in_specs=[
            pl.BlockSpec((tm, d), lambda i: (i, 0)),
            pl.BlockSpec((None, tm, p.shape[2]), lambda i: (layer, i, 0)),
            _resident(wgate.shape),
            _resident(wproj.shape, layer),
        ],
        out_specs=pl.BlockSpec((tm, d), lambda i: (i, 0)),
        out_shape=jax.ShapeDtypeStruct((m, d), F32),
        compiler_params=_params("parallel"),
        name="ple",
    )(x, p, wgate, wproj)


N_MLP_IN = 5
N_PLE_IN = 3


def _ffn_kernel(*refs, nf, with_ple, emit_bf16):
    x_ref, gpre_ref, gpost_ref, wup_ref, wdown_ref = refs[:N_MLP_IN]
    n_in = N_MLP_IN + (N_PLE_IN if with_ple else 0)
    if with_ple:
        p_ref, wgate_ref, wproj_ref = refs[N_MLP_IN:n_in]
    n_emit = 2 if emit_bf16 else 0
    n_cast = (len(refs) - n_in - 2 - n_emit) // 2
    cast_in = refs[n_in:n_in + n_cast]
    y_ref = refs[n_in + n_cast]
    cast_out = refs[n_in + n_cast + 1:n_in + 2 * n_cast + 1]
    h_scr = refs[-1]
    _run_casts(cast_in, cast_out)
    f = pl.program_id(1)

    def mlp_part(h):
        wup, wdown = wup_ref[...], wdown_ref[...]
        if emit_bf16:
            wup_out, wdown_out = refs[-3:-1]
            wup, wdown = wup.astype(BF16), wdown.astype(BF16)
            wup_out[...] = wup
            wdown_out[...] = wdown
        u = jnp.maximum(jnp.dot(h, wup, preferred_element_type=F32), 0.0)
        return jnp.dot((u * u).astype(BF16), wdown, preferred_element_type=F32)

    @pl.when(f == 0)
    def _():
        h = _rms(x_ref[...], gpre_ref[...]).astype(BF16)
        h_scr[...] = h
        y_ref[...] = mlp_part(h)

    @pl.when(jnp.logical_and(f > 0, f < nf - 1))
    def _():
        y_ref[...] += mlp_part(h_scr[...])

    @pl.when(f == nf - 1)
    def _():
        x1 = x_ref[...] + _rms(y_ref[...] + mlp_part(h_scr[...]), gpost_ref[...])
        if with_ple:
            emb = jnp.dot(p_ref[...].astype(BF16), wproj_ref[...], preferred_element_type=F32)
            gate = jax.nn.sigmoid(jnp.dot(x1.astype(BF16), wgate_ref[...], preferred_element_type=F32))
            x1 = x1 + gate * emb
        y_ref[...] = x1


def _ffn(x, gpre, gpost, wup, wdown, *, tm, tf, ple=None, casts=(), f32_layer=None):
    m, d = x.shape
    dff = wup.shape[-1]
    nf = dff // tf
    assert nf >= 2
    emit = f32_layer is not None
    assert not emit or m == tm
    if emit:
        w_specs = [pl.BlockSpec((None, d, tf), lambda i, f: (f32_layer, 0, f)),
                   pl.BlockSpec((None, tf, d), lambda i, f: (f32_layer, f, 0))]
        emit_specs = [pl.BlockSpec((d, tf), lambda i, f: (0, f)), pl.BlockSpec((tf, d), lambda i, f: (f, 0))]
        emit_shapes = [jax.ShapeDtypeStruct((d, dff), BF16), jax.ShapeDtypeStruct((dff, d), BF16)]
    else:
        w_specs = [pl.BlockSpec((d, tf), lambda i, f: (0, f)), pl.BlockSpec((tf, d), lambda i, f: (f, 0))]
        emit_specs, emit_shapes = [], []
    c_in, c_out, c_shapes, c_ops = _cast_specs(casts, (m // tm) * nf, lambda i, f: i * nf + f)
    ple_specs, ple_ops = [], []
    if ple is not None:
        p, wgate, wproj, layer = ple
        ple_specs = [
            pl.BlockSpec((None, tm, p.shape[2]), lambda i, f: (layer, i, 0)),
            _resident(wgate.shape),
            _resident(wproj.shape, layer),
        ]
        ple_ops = [p, wgate, wproj]
    return pl.pallas_call(
        functools.partial(_ffn_kernel, nf=nf, with_ple=ple is not None, emit_bf16=emit),
        grid=(m // tm, nf),
        in_specs=[
            pl.BlockSpec((tm, d), lambda i, f: (i, 0)),
            _resident((1, d)),
            _resident((1, d)),
        ] + w_specs + ple_specs + c_in,
        out_specs=[pl.BlockSpec((tm, d), lambda i, f: (i, 0))] + c_out + emit_specs,
        out_shape=[jax.ShapeDtypeStruct((m, d), F32)] + c_shapes + emit_shapes,
        scratch_shapes=[pltpu.VMEM((tm, d), BF16)],
        compiler_params=_params("parallel", "arbitrary"),
        name="ffn_ple" if ple is not None else "ffn",
    )(x, gpre, gpost, wup, wdown, *ple_ops, *c_ops)


def _row(v):
    return v.reshape(1, -1)


def kernel(x_prompt, x_sample, cache_k, cache_v, state_pool, p_prompt, p_sample, norm_mix_pre, norm_mix_post, norm_ffn_pre, norm_ffn_post, w_qkv, b_qkv, w_o, sinks, w_pool, pool_scale, w_ffn_up, w_ffn_down, w_ple_proj, w_ple_gate):
    b, s, d = x_prompt.shape
    bs, ts, _ = x_sample.shape
    q_dim = w_o.shape[1]
    kv_dim = (w_qkv.shape[2] - q_dim) // 2
    n_kv = kv_dim // HEAD_DIM
    tm = TILES.rows
    assert s % tm == 0 and (bs * ts) % tm == 0 and tm % ts == 0

    def bf(w):
        return w.astype(BF16)

    xp = x_prompt.reshape(b * s, d)
    xs = x_sample.reshape(bs * ts, d)
    wproj = bf(w_ple_proj)
    pp = p_prompt.reshape(p_prompt.shape[0], b * s, -1)
    ps = p_sample.reshape(p_sample.shape[0], bs * ts, -1)

    def ffn(x, i, wup, wdown, *, tf, p=None, wgate=None, casts=()):
        ple = None if p is None else (p, wgate, wproj, i)
        return _ffn(x, _row(norm_ffn_pre[i]), _row(norm_ffn_post[i]), wup, wdown, tm=tm, tf=tf, ple=ple, casts=casts)

    tab_p = _rope_tables(jnp.arange(s, dtype=jnp.int32))
    tab_s = tuple(jnp.tile(t, (tm // ts, 1)) for t in _rope_tables(PAST_LEN + jnp.arange(ts, dtype=jnp.int32)))
    wqkv, bqkv = bf(w_qkv[0]), _row(b_qkv[0])
    g_pre, g_post = _row(norm_mix_pre[0]), _row(norm_mix_post[0])

    q_p, k_p, v_p, *rest = _qkv_rope(xp, g_pre, wqkv, bqkv, tab_p, tm=TILES.rows_wide, q_dim=q_dim, kv_dim=kv_dim,
                                     casts=((w_ple_gate, 0), (w_o, 0)))
    kv_p, (wgate0, wo) = rest[:4], rest[4:]
    q_s, k_s, v_s, *kv_s = _qkv_rope(xs, g_pre, wqkv, bqkv, tab_s, tm=tm, q_dim=q_dim, kv_dim=kv_dim)

    k_p3, v_p3 = k_p.reshape(b, s, kv_dim), v_p.reshape(b, s, kv_dim)
    k_s3, v_s3 = k_s.reshape(bs, ts, kv_dim), v_s.reshape(bs, ts, kv_dim)
    o_p = _attention(q_p.reshape(b, s, q_dim), *(t.reshape(b, s, -1) for t in kv_p), sinks[0], tq=TILES.attn_rows)
    cache = (*_pair_layouts(cache_k[0].reshape(bs, WINDOW, kv_dim), n_kv),
             *_pair_layouts(cache_v[0].reshape(bs, WINDOW, kv_dim), n_kv))
    kv_s = (jnp.concatenate([c, t.reshape(bs, ts, -1)], axis=1) for c, t in zip(cache, kv_s))
    o_s = _attention(q_s.reshape(bs, ts, q_dim), *kv_s, sinks[0], tq=ts)

    xp = _out_proj(o_p.reshape(b * s, q_dim), xp, wo, g_post, tm=TILES.rows_wide)
    xs = _out_proj(o_s.reshape(bs * ts, q_dim), xs, wo, g_post, tm=tm)

    xs, wup0, wdown0 = _ffn(xs, _row(norm_ffn_pre[0]), _row(norm_ffn_post[0]), w_ffn_up, w_ffn_down, tm=tm,
                            tf=TILES.dff_f32, ple=(ps, wgate0, wproj, 0), f32_layer=0)
    xp, wup1, wdown1, wgate1 = ffn(xp, 0, wup0, wdown0, tf=TILES.dff,
                                   casts=((w_ffn_up, 1), (w_ffn_down, 1), (w_ple_gate, 1)))

    g_pre, g_post = _row(norm_mix_pre[1]), _row(norm_mix_post[1])
    wpool, pscale = bf(w_pool[0]), _row(pool_scale[0])
    xp3, tail_p = _ple_pool(xp.reshape(b, s, d), pp, wgate0, wproj, g_pre, g_post, wpool, pscale,
                            layer=0, tm=tm)
    hist = jnp.pad(state_pool[0], ((0, 0), (HALO - POOL_HIST, 0), (0, 0)))
    xs3, tail_s = _pool_mixer(xs.reshape(bs, ts, d), hist, g_pre, g_post, wpool, pscale, pos0=PAST_LEN)

    (xp,) = ffn(xp3.reshape(b * s, d), 1, wup1, wdown1, tf=TILES.dff)
    xp = _ple(xp, pp, wgate1, wproj, layer=1, tm=TILES.rows_wide)
    (xs,) = ffn(xs3.reshape(bs * ts, d), 1, wup1, wdown1, tf=TILES.dff_ple, p=ps, wgate=wgate1)

    n_heads_kv = (n_kv, HEAD_DIM)
    return (
        xp.reshape(b, s, d),
        xs.reshape(bs, ts, d),
        k_p3[:, s - WINDOW:].reshape(1, b, WINDOW, *n_heads_kv),
        v_p3[:, s - WINDOW:].reshape(1, b, WINDOW, *n_heads_kv),
        tail_p[:, -1, HALO - POOL_HIST:][None],
        k_s3.reshape(1, bs, ts, *n_heads_kv),
        v_s3.reshape(1, bs, ts, *n_heads_kv),
        tail_s[:, HALO - POOL_HIST:][None],
    )
```

```python
import functools
import math
from typing import NamedTuple

import jax
import jax.numpy as jnp
from jax import lax
from jax.experimental import pallas as pl
from jax.experimental.pallas import tpu as pltpu

HEAD_DIM = 64
GQA_GROUP = 8
CHUNK = 64
WINDOW = 128
ROT_DIM = HEAD_DIM // 4
ROPE_THETA = 500000.0
POOL_WINDOWS = (2, 4, 8, 16)
POOL_HIST = max(POOL_WINDOWS) - 1
PAST_LEN = 2048
EPS = 1e-6
LOG2E = math.log2(math.e)

LANES = 128
BF16_ROWS = 16
HALO = 16
SUB_ROWS = 256
VMEM_LIMIT_BYTES = 60 * 1024 * 1024


class _Tiles(NamedTuple):
    rows: int = 512
    rows_wide: int = 1024
    attn_rows: int = 1024
    dff: int = 2048
    dff_ple: int = 1024


TILES = _Tiles()

F32 = jnp.float32
BF16 = jnp.bfloat16


def _rms(x, g):
    ms = jnp.mean(x * x, axis=-1, keepdims=True)
    return x * lax.rsqrt(ms + EPS) * g


def _resident(shape, layer=None):
    if layer is None:
        nd = len(shape)
        return pl.BlockSpec(shape, lambda *_: (0,) * nd, pipeline_mode=pl.Buffered(1))
    nd = len(shape) - 1
    return pl.BlockSpec((None,) + tuple(shape[1:]), lambda *_: (layer,) + (0,) * nd, pipeline_mode=pl.Buffered(1))


def _params(*sem):
    return pltpu.CompilerParams(dimension_semantics=sem, vmem_limit_bytes=VMEM_LIMIT_BYTES)


def _cast_specs(casts, n_steps, flat_step):
    in_specs, out_specs, out_shapes, operands = [], [], [], []
    for w, layer in casts:
        _, r, c = w.shape
        rb = max(BF16_ROWS, r // n_steps)
        csplit = rb * n_steps // r
        assert r % rb == 0 and (r // rb) * csplit == n_steps and c % (csplit * LANES) == 0
        cw = c // csplit

        def in_map(*idx, layer=layer, csplit=csplit):
            k = flat_step(*idx)
            return (layer, k // csplit, k % csplit)

        def out_map(*idx, csplit=csplit):
            k = flat_step(*idx)
            return (k // csplit, k % csplit)

        in_specs.append(pl.BlockSpec((None, rb, cw), in_map))
        out_specs.append(pl.BlockSpec((rb, cw), out_map))
        out_shapes.append(jax.ShapeDtypeStruct((r, c), BF16))
        operands.append(w)
    return in_specs, out_specs, out_shapes, operands


def _run_casts(srcs, dsts):
    for src, dst in zip(srcs, dsts, strict=True):
        dst[...] = src[...].astype(BF16)


def _rope_tables(positions):
    half = ROT_DIM // 2
    inv = ROPE_THETA ** (-jnp.arange(0, ROT_DIM, 2, dtype=F32) / ROT_DIM)
    ang = positions.astype(F32)[:, None] * inv[None, :]
    cos, sin = jnp.cos(ang), jnp.sin(ang)
    p = positions.shape[0]
    ones = jnp.ones((p, HEAD_DIM - ROT_DIM), F32)
    zeros = jnp.zeros((p, HEAD_DIM - ROT_DIM), F32)
    zh = jnp.zeros((p, half), F32)
    c = jnp.concatenate([cos, cos, ones], axis=1)
    sa = jnp.concatenate([-sin, zh, zeros], axis=1)
    sb = jnp.concatenate([zh, sin, zeros], axis=1)
    rep = LANES // HEAD_DIM
    return tuple(jnp.tile(t, (1, rep)) for t in (c, sa, sb))


def _store_pair_layouts(blk, a_ref, b_ref, rows, col):
    lo = lax.broadcasted_iota(jnp.int32, blk.shape, 1) < HEAD_DIM
    swapped = pltpu.roll(blk, HEAD_DIM, 1)
    zero = jnp.zeros_like(blk)
    a_ref[rows, col:col + LANES] = jnp.where(lo, blk, zero).astype(BF16)
    b_ref[rows, col:col + LANES] = jnp.where(lo, zero, swapped).astype(BF16)
    a_ref[rows, col + LANES:col + 2 * LANES] = jnp.where(lo, swapped, zero).astype(BF16)
    b_ref[rows, col + LANES:col + 2 * LANES] = jnp.where(lo, zero, blk).astype(BF16)


N_QKV_IN = 7
N_QKV_OUT = 7


def _qkv_kernel(*refs, q_dim, kv_dim, nchunk):
    x_ref, g_ref, w_ref, b_ref, c_ref, sa_ref, sb_ref = refs[:N_QKV_IN]
    n_cast = (len(refs) - N_QKV_IN - N_QKV_OUT) // 2
    cast_in = refs[N_QKV_IN:N_QKV_IN + n_cast]
    q_ref, k_ref, v_ref, ka_ref, kb_ref, va_ref, vb_ref = refs[N_QKV_IN + n_cast:N_QKV_IN + n_cast + N_QKV_OUT]
    cast_out = refs[N_QKV_IN + n_cast + N_QKV_OUT:]
    _run_casts(cast_in, cast_out)

    scale = LOG2E / math.sqrt(HEAD_DIM)
    half = ROT_DIM // 2
    heads_per_blk = LANES // HEAD_DIM
    rot_cols = q_dim + kv_dim
    total = q_dim + 2 * kv_dim
    for r0 in range(0, x_ref.shape[0], SUB_ROWS):
        rows = slice(r0, r0 + SUB_ROWS)
        h = _rms(x_ref[rows, :], g_ref[...]).astype(BF16)
        c, sa, sb = c_ref[rows, :], sa_ref[rows, :], sb_ref[rows, :]
        cq, saq, sbq = c * scale, sa * scale, sb * scale
        for n0 in range(0, total, nchunk):
            acc = jnp.dot(h, w_ref[:, n0:n0 + nchunk], preferred_element_type=F32) + b_ref[:, n0:n0 + nchunk]
            for j0 in range(0, nchunk, LANES):
                col = n0 + j0
                blk = acc[:, j0:j0 + LANES]
                if col < q_dim:
                    blk = blk * cq + pltpu.roll(blk, LANES - half, 1) * saq + pltpu.roll(blk, half, 1) * sbq
                    q_ref[rows, col:col + LANES] = blk.astype(BF16)
                elif col < rot_cols:
                    blk = blk * c + pltpu.roll(blk, LANES - half, 1) * sa + pltpu.roll(blk, half, 1) * sb
                    k_ref[rows, col - q_dim:col - q_dim + LANES] = blk
                    _store_pair_layouts(blk, ka_ref, kb_ref, rows, (col - q_dim) * heads_per_blk)
                else:
                    v_ref[rows, col - rot_cols:col - rot_cols + LANES] = blk
                    _store_pair_layouts(blk, va_ref, vb_ref, rows, (col - rot_cols) * heads_per_blk)


def _qkv_rope(x, g, w_bf16, b, tables, *, tm, q_dim, kv_dim, casts=()):
    m, d = x.shape
    total = q_dim + 2 * kv_dim
    ptiles = tables[0].shape[0] // tm
    tab_spec = pl.BlockSpec((tm, LANES), lambda i: (i % ptiles, 0))
    kern = functools.partial(_qkv_kernel, q_dim=q_dim, kv_dim=kv_dim, nchunk=256)
    pair_w = kv_dim * (LANES // HEAD_DIM)
    c_in, c_out, c_shapes, c_ops = _cast_specs(casts, m // tm, lambda i: i)

    def rows(width):
        return pl.BlockSpec((tm, width), lambda i: (i, 0))

    return pl.pallas_call(
        kern,
        grid=(m // tm,),
        in_specs=[
            rows(d),
            _resident((1, d)),
            _resident((d, total)),
            _resident((1, total)),
            tab_spec, tab_spec, tab_spec,
        ] + c_in,
        out_specs=[rows(q_dim), rows(kv_dim), rows(kv_dim), rows(pair_w), rows(pair_w), rows(pair_w), rows(pair_w)]
        + c_out,
        out_shape=[
            jax.ShapeDtypeStruct((m, q_dim), BF16),
            jax.ShapeDtypeStruct((m, kv_dim), F32),
            jax.ShapeDtypeStruct((m, kv_dim), F32),
        ] + [jax.ShapeDtypeStruct((m, pair_w), BF16)] * 4 + c_shapes,
        compiler_params=_params("parallel"),
        name="qkv_rope",
    )(x, g, w_bf16, b, *tables, *c_ops)


def _chunks_per_body(chunks):
    return next(g for g in (4, 2, 1) if chunks % g == 0)


def _attn_kernel(sink_ref, q_ref, ka_ref, kb_ref, va_ref, vb_ref, o_ref, s_scr, *, n_kv, chunks, group, hist_chunks):
    step = pl.program_id(1)
    span = WINDOW + CHUNK
    pairs = GQA_GROUP // 2
    win_chunks = WINDOW // CHUNK
    neg_inf = -jnp.inf
    row_k = lax.broadcasted_iota(jnp.int32, (2 * span, LANES), 0)
    lane_k = lax.broadcasted_iota(jnp.int32, (2 * span, LANES), 1)
    ones_ab = jnp.where((row_k < span) == (lane_k < HEAD_DIM), 1.0, 0.0).astype(BF16)
    lo = lax.broadcasted_iota(jnp.int32, (CHUNK, LANES), 1) < HEAD_DIM
    nt = (((1,), (1,)), ((), ()))

    def body(j, carry, *, masked):
        units = []
        for u in range(group):
            ci = j * group + u
            ckv = step * chunks + ci + hist_chunks
            k0 = pl.multiple_of(jnp.maximum(ckv - win_chunks, 0) * CHUNK, CHUNK)
            units.append((pl.ds(pl.multiple_of(ci * CHUNK, CHUNK), CHUNK), pl.ds(k0, span), ckv))

        for u, (rows, win, _) in enumerate(units):
            for kh in range(n_kv):
                kcols = slice(kh * LANES, (kh + 1) * LANES)
                q_pairs = jnp.concatenate(
                    [q_ref[rows, (kh * pairs + p) * LANES:(kh * pairs + p + 1) * LANES] for p in range(pairs)], axis=0)
                k_ab = jnp.concatenate([ka_ref[win, kcols], kb_ref[win, kcols]], axis=0)
                s_scr[u, kh] = lax.dot_general(q_pairs, k_ab, nt, preferred_element_type=F32)

        e_all, t_all = [], []
        for u, (_, _, ckv) in enumerate(units):
            if masked:
                key_row = lax.broadcasted_iota(jnp.int32, (1, span), 1)
                bias1 = jnp.where(key_row < (ckv + 1) * CHUNK, 0.0, neg_inf)
                bias = jnp.concatenate([bias1, bias1], axis=1)
            for kh in range(n_kv):
                e_rows, t_rows = [], []
                for p in range(pairs):
                    s = s_scr[u, kh, p * CHUNK:(p + 1) * CHUNK, :]
                    if masked:
                        s = s + bias
                    t0, t1, t2 = s[:, :LANES], s[:, LANES:2 * LANES], s[:, 2 * LANES:]
                    sk_a = sink_ref[kh * GQA_GROUP + 2 * p] * LOG2E
                    sk_b = sink_ref[kh * GQA_GROUP + 2 * p + 1] * LOG2E
                    m_a = jnp.max(jnp.maximum(t0, jnp.where(lo, t1, neg_inf)), axis=-1, keepdims=True)
                    m_b = jnp.max(jnp.maximum(t2, jnp.where(lo, neg_inf, t1)), axis=-1, keepdims=True)
                    m_a, m_b = jnp.maximum(m_a, sk_a), jnp.maximum(m_b, sk_b)
                    m_ab = jnp.where(lo, m_a, m_b)
                    e = jnp.concatenate([jnp.exp2(t0 - m_a), jnp.exp2(t1 - m_ab), jnp.exp2(t2 - m_b)], axis=1)
                    e_rows.append(e.astype(BF16))
                    t_rows.append(jnp.exp2(jnp.where(lo[:1], sk_a, sk_b) - m_ab))
                e_all.append(jnp.concatenate(e_rows, axis=0))
                t_all.append(jnp.concatenate(t_rows, axis=0))

        for u, (rows, win, _) in enumerate(units):
            for kh in range(n_kv):
                kcols = slice(kh * LANES, (kh + 1) * LANES)
                v_ab = jnp.concatenate([va_ref[win, kcols], vb_ref[win, kcols]], axis=0)
                rhs = jnp.concatenate([v_ab, ones_ab], axis=1)
                acc = jnp.dot(e_all[u * n_kv + kh], rhs, preferred_element_type=F32)
                o_pairs = (acc[:, :LANES] / (acc[:, LANES:] + t_all[u * n_kv + kh])).astype(BF16)
                for p in range(pairs):
                    col = (kh * pairs + p) * LANES
                    o_ref[rows, col:col + LANES] = o_pairs[p * CHUNK:(p + 1) * CHUNK, :]
        return carry

    def run(masked):
        lax.fori_loop(0, chunks // group, functools.partial(body, masked=masked), 0)

    if hist_chunks >= win_chunks:
        run(False)
    else:
        needs_mask = step * chunks + hist_chunks < win_chunks
        pl.when(needs_mask)(lambda: run(True))
        pl.when(jnp.logical_not(needs_mask))(lambda: run(False))


def _attention(q, ka, kb, va, vb, sinks, *, tq):
    b, s, q_dim = q.shape
    rows, kvw = ka.shape[1:]
    n_kv = kvw // LANES
    span = WINDOW + CHUNK
    chunks = tq // CHUNK
    assert rows >= span and (rows - s) % CHUNK == 0
    hist_spec = pl.BlockSpec((None, rows, kvw), lambda i, c: (i, 0, 0))
    group = _chunks_per_body(chunks)
    kern = functools.partial(_attn_kernel, n_kv=n_kv, chunks=chunks, group=group, hist_chunks=(rows - s) // CHUNK)
    return pl.pallas_call(
        kern,
        grid=(b, s // tq),
        in_specs=[
            pl.BlockSpec(memory_space=pltpu.SMEM),
            pl.BlockSpec((None, tq, q_dim), lambda i, c: (i, c, 0)),
            hist_spec, hist_spec, hist_spec, hist_spec,
        ],
        out_specs=pl.BlockSpec((None, tq, q_dim), lambda i, c: (i, c, 0)),
        out_shape=jax.ShapeDtypeStruct((b, s, q_dim), BF16),
        scratch_shapes=[pltpu.VMEM((group, n_kv, (GQA_GROUP // 2) * CHUNK, 2 * span), F32)],
        compiler_params=_params("parallel", "arbitrary"),
        name="swa_attention",
    )(sinks, q, ka, kb, va, vb)


def _pair_layouts(x3, n_kv):
    b, r, _ = x3.shape
    x4 = x3.astype(BF16).reshape(b, r, n_kv, HEAD_DIM)
    lo = jnp.pad(x4, ((0, 0), (0, 0), (0, 0), (0, LANES - HEAD_DIM)))
    hi = jnp.pad(x4, ((0, 0), (0, 0), (0, 0), (LANES - HEAD_DIM, 0)))
    return lo.reshape(b, r, n_kv * LANES), hi.reshape(b, r, n_kv * LANES)


def _oproj_kernel(o_ref, x_ref, w_ref, g_ref, y_ref):
    mix = jnp.dot(o_ref[...], w_ref[...], preferred_element_type=F32)
    y_ref[...] = x_ref[...] + _rms(mix, g_ref[...])


def _out_proj(o, x, w_bf16, g, *, tm):
    m, d = x.shape
    return pl.pallas_call(
        _oproj_kernel,
        grid=(m // tm,),
        in_specs=[
            pl.BlockSpec((tm, o.shape[1]), lambda i: (i, 0)),
            pl.BlockSpec((tm, d), lambda i: (i, 0)),
            _resident(w_bf16.shape),
            _resident((1, d)),
        ],
        out_specs=pl.BlockSpec((tm, d), lambda i: (i, 0)),
        out_shape=jax.ShapeDtypeStruct((m, d), F32),
        compiler_params=_params("parallel"),
        name="attn_out_proj",
    )(o, x, w_bf16, g)


def _pool_group(h, prev, pos, w_ref, gi):
    w = POOL_WINDOWS[gi]
    group_dim = h.shape[1] // len(POOL_WINDOWS)
    sl = slice(gi * group_dim, (gi + 1) * group_dim)
    hg = h[:, sl]
    acc = jnp.concatenate([prev[:, sl], hg], axis=0)
    shift = 1
    while shift < w:
        acc = acc + pltpu.roll(acc, shift, 0)
        shift *= 2
    cnt = jnp.minimum(pos + 1, w).astype(F32)
    dgrp = (acc[HALO:, :] / cnt - hg).astype(BF16)
    return jnp.dot(dgrp, w_ref[gi], preferred_element_type=F32)


def _pool_mix(h, prev, pos, w_ref, scale):
    outs = [_pool_group(h, prev, pos, w_ref, gi) for gi in range(len(POOL_WINDOWS))]
    return jnp.concatenate(outs, axis=1) * scale


def _pool_kernel(x_ref, hist_ref, gpre_ref, gpost_ref, w_ref, scale_ref, y_ref, tail_ref, *, pos0):
    x = x_ref[...]
    rows = x.shape[0]
    h = _rms(x, gpre_ref[...])
    tail_ref[...] = h[rows - HALO:, :]
    pos = pos0 + lax.broadcasted_iota(jnp.int32, (rows, 1), 0)
    mix = _pool_mix(h, hist_ref[...], pos, w_ref, scale_ref[...])
    y_ref[...] = x + _rms(mix, gpost_ref[...])


def _pool_mixer(x, hist, gpre, gpost, w_bf16, scale, *, pos0):
    b, t, d = x.shape
    return pl.pallas_call(
        functools.partial(_pool_kernel, pos0=pos0),
        grid=(b,),
        in_specs=[
            pl.BlockSpec((None, t, d), lambda bi: (bi, 0, 0)),
            pl.BlockSpec((None, HALO, d), lambda bi: (bi, 0, 0)),
            _resident((1, d)),
            _resident((1, d)),
            _resident(w_bf16.shape),
            _resident((1, d)),
        ],
        out_specs=[
            pl.BlockSpec((None, t, d), lambda bi: (bi, 0, 0)),
            pl.BlockSpec((None, HALO, d), lambda bi: (bi, 0, 0)),
        ],
        out_shape=[
            jax.ShapeDtypeStruct((b, t, d), F32),
            jax.ShapeDtypeStruct((b, HALO, d), F32),
        ],
        compiler_params=_params("parallel"),
        name="pool_mixer",
    )(x, hist, gpre, gpost, w_bf16, scale)


def _ple_pool_kernel(x_ref, p_ref, wgate_ref, wproj_ref, gpre_ref, gpost_ref, w_ref, scale_ref,
                     y_ref, tail_ref, even_scr, odd_scr, halo_scr, *, tm, tiles_per_seq, n_tiles):
    k = pl.program_id(0)

    @pl.when(k == 0)
    def _():
        odd_scr[...] = jnp.zeros_like(odd_scr)
        halo_scr[...] = jnp.zeros_like(halo_scr)

    def step(dst_scr, src_scr):
        n_groups = len(POOL_WINDOWS)
        n_chunks = 2 * n_groups
        if dst_scr is not None:
            x = x_ref[...]
            xb, pb = x.astype(BF16), p_ref[...].astype(BF16)
            cw = x.shape[1] // n_chunks

        def embed_chunk(c):
            if dst_scr is None:
                return
            cols = slice(c * cw, (c + 1) * cw)
            gate = jax.nn.sigmoid(jnp.dot(xb, wgate_ref[:, cols], preferred_element_type=F32))
            emb = jnp.dot(pb, wproj_ref[:, cols], preferred_element_type=F32)
            dst_scr[:, cols] = x[:, cols] + gate * emb

        embed_chunk(0)
        seq_tile = lax.rem(jnp.maximum(k - 1, 0), tiles_per_seq)
        x1 = src_scr[...]
        h = _rms(x1, gpre_ref[...])
        prev = jnp.where(seq_tile == 0, 0.0, halo_scr[...])
        pos = seq_tile * tm + lax.broadcasted_iota(jnp.int32, (tm, 1), 0)
        tail = h[tm - HALO:, :]
        halo_scr[...] = tail
        tail_ref[...] = tail
        outs = []
        for gi in range(n_groups):
            outs.append(_pool_group(h, prev, pos, w_ref, gi))
            embed_chunk(gi + 1)
        mix = jnp.concatenate(outs, axis=1) * scale_ref[...]
        y_ref[...] = x1 + _rms(mix, gpost_ref[...])
        for c in range(n_groups + 1, n_chunks):
            embed_chunk(c)

    is_even = lax.rem(k, 2) == 0
    is_last = k == n_tiles
    last_src = even_scr if (n_tiles - 1) % 2 == 0 else odd_scr
    pl.when(jnp.logical_and(is_even, jnp.logical_not(is_last)))(lambda: step(even_scr, odd_scr))
    pl.when(jnp.logical_and(jnp.logical_not(is_even), jnp.logical_not(is_last)))(lambda: step(odd_scr, even_scr))
    pl.when(is_last)(lambda: step(None, last_src))


def _ple_pool(x, p, wgate, wproj, gpre, gpost, w_bf16, scale, *, layer, tm):
    b, s, d = x.shape
    nt = s // tm
    n_tiles = b * nt

    def cur(k):
        return jnp.minimum(k, n_tiles - 1)

    def lagged(k):
        return jnp.maximum(k - 1, 0)

    return pl.pallas_call(
        functools.partial(_ple_pool_kernel, tm=tm, tiles_per_seq=nt, n_tiles=n_tiles),
        grid=(n_tiles + 1,),
        in_specs=[
            pl.BlockSpec((None, tm, d), lambda k: (cur(k) // nt, cur(k) % nt, 0)),
            pl.BlockSpec((None, tm, p.shape[2]), lambda k: (layer, cur(k), 0)),
            _resident(wgate.shape),
            _resident(wproj.shape, layer),
            _resident((1, d)),
            _resident((1, d)),
            _resident(w_bf16.shape),
            _resident((1, d)),
        ],
        out_specs=[
            pl.BlockSpec((None, tm, d), lambda k: (lagged(k) // nt, lagged(k) % nt, 0)),
            pl.BlockSpec((None, None, HALO, d), lambda k: (lagged(k) // nt, lagged(k) % nt, 0, 0)),
        ],
        out_shape=[
            jax.ShapeDtypeStruct((b, s, d), F32),
            jax.ShapeDtypeStruct((b, nt, HALO, d), F32),
        ],
        scratch_shapes=[pltpu.VMEM((tm, d), F32), pltpu.VMEM((tm, d), F32), pltpu.VMEM((HALO, d), F32)],
        compiler_params=_params("arbitrary"),
        name="ple_pool",
    )(x, p, wgate, wproj, gpre, gpost, w_bf16, scale)


def _ple_kernel(x_ref, p_ref, wgate_ref, wproj_ref, y_ref):
    for r0 in range(0, x_ref.shape[0], SUB_ROWS):
        rows = slice(r0, r0 + SUB_ROWS)
        x = x_ref[rows, :]
        gate = jax.nn.sigmoid(jnp.dot(x.astype(BF16), wgate_ref[...], preferred_element_type=F32))
        emb = jnp.dot(p_ref[rows, :].astype(BF16), wproj_ref[...], preferred_element_type=F32)
        y_ref[rows, :] = x + gate * emb


def _ple(x, p, wgate, wproj, *, layer, tm):
    m, d = x.shape
    return pl.pallas_call(
        _ple_kernel,
        grid=(m // tm,),
        in_specs=[
            pl.BlockSpec((tm, d), lambda i: (i, 0)),
            pl.BlockSpec((None, tm, p.shape[2]), lambda i: (layer, i, 0)),
            _resident(wgate.shape),
            _resident(wproj.shape, layer),
        ],
        out_specs=pl.BlockSpec((tm, d), lambda i: (i, 0)),
        out_shape=jax.ShapeDtypeStruct((m, d), F32),
        compiler_params=_params("parallel"),
        name="ple",
    )(x, p, wgate, wproj)


N_MLP_IN = 5
N_PLE_IN = 3


def _ffn_kernel(*refs, nf, with_ple):
    x_ref, gpre_ref, gpost_ref, wup_ref, wdown_ref = refs[:N_MLP_IN]
    n_in = N_MLP_IN + (N_PLE_IN if with_ple else 0)
    if with_ple:
        p_ref, wgate_ref, wproj_ref = refs[N_MLP_IN:n_in]
    n_cast = (len(refs) - n_in - 2) // 2
    cast_in = refs[n_in:n_in + n_cast]
    y_ref = refs[n_in + n_cast]
    cast_out = refs[n_in + n_cast + 1:n_in + 2 * n_cast + 1]
    h_scr = refs[-1]
    _run_casts(cast_in, cast_out)
    f = pl.program_id(1)

    def mlp_part(h):
        u = jnp.maximum(jnp.dot(h, wup_ref[...], preferred_element_type=F32), 0.0)
        return jnp.dot((u * u).astype(BF16), wdown_ref[...], preferred_element_type=F32)

    @pl.when(f == 0)
    def _():
        h = _rms(x_ref[...], gpre_ref[...]).astype(BF16)
        h_scr[...] = h
        y_ref[...] = mlp_part(h)

    @pl.when(jnp.logical_and(f > 0, f < nf - 1))
    def _():
        y_ref[...] += mlp_part(h_scr[...])

    @pl.when(f == nf - 1)
    def _():
        x1 = x_ref[...] + _rms(y_ref[...] + mlp_part(h_scr[...]), gpost_ref[...])
        if with_ple:
            emb = jnp.dot(p_ref[...].astype(BF16), wproj_ref[...], preferred_element_type=F32)
            gate = jax.nn.sigmoid(jnp.dot(x1.astype(BF16), wgate_ref[...], preferred_element_type=F32))
            x1 = x1 + gate * emb
        y_ref[...] = x1


def _ffn(x, gpre, gpost, wup, wdown, *, tm, tf, ple=None, casts=()):
    m, d = x.shape
    dff = wup.shape[1]
    nf = dff // tf
    assert nf >= 2
    c_in, c_out, c_shapes, c_ops = _cast_specs(casts, (m // tm) * nf, lambda i, f: i * nf + f)
    ple_specs, ple_ops = [], []
    if ple is not None:
        p, wgate, wproj, layer = ple
        ple_specs = [
            pl.BlockSpec((None, tm, p.shape[2]), lambda i, f: (layer, i, 0)),
            _resident(wgate.shape),
            _resident(wproj.shape, layer),
        ]
        ple_ops = [p, wgate, wproj]
    return pl.pallas_call(
        functools.partial(_ffn_kernel, nf=nf, with_ple=ple is not None),
        grid=(m // tm, nf),
        in_specs=[
            pl.BlockSpec((tm, d), lambda i, f: (i, 0)),
            _resident((1, d)),
            _resident((1, d)),
            pl.BlockSpec((d, tf), lambda i, f: (0, f)),
            pl.BlockSpec((tf, d), lambda i, f: (f, 0)),
        ] + ple_specs + c_in,
        out_specs=[pl.BlockSpec((tm, d), lambda i, f: (i, 0))] + c_out,
        out_shape=[jax.ShapeDtypeStruct((m, d), F32)] + c_shapes,
        scratch_shapes=[pltpu.VMEM((tm, d), BF16)],
        compiler_params=_params("parallel", "arbitrary"),
        name="ffn_ple" if ple is not None else "ffn",
    )(x, gpre, gpost, wup, wdown, *ple_ops, *c_ops)


def _row(v):
    return v.reshape(1, -1)


def kernel(x_prompt, x_sample, cache_k, cache_v, state_pool, p_prompt, p_sample, norm_mix_pre, norm_mix_post, norm_ffn_pre, norm_ffn_post, w_qkv, b_qkv, w_o, sinks, w_pool, pool_scale, w_ffn_up, w_ffn_down, w_ple_proj, w_ple_gate):
    b, s, d = x_prompt.shape
    bs, ts, _ = x_sample.shape
    q_dim = w_o.shape[1]
    kv_dim = (w_qkv.shape[2] - q_dim) // 2
    n_kv = kv_dim // HEAD_DIM
    tm = TILES.rows
    assert s % tm == 0 and (bs * ts) % tm == 0 and tm % ts == 0

    def bf(w):
        return w.astype(BF16)

    xp = x_prompt.reshape(b * s, d)
    xs = x_sample.reshape(bs * ts, d)
    wproj = bf(w_ple_proj)
    pp = p_prompt.reshape(p_prompt.shape[0], b * s, -1)
    ps = p_sample.reshape(p_sample.shape[0], bs * ts, -1)

    def ffn(x, i, wup, wdown, *, tf, p=None, wgate=None, casts=()):
        ple = None if p is None else (p, wgate, wproj, i)
        return _ffn(x, _row(norm_ffn_pre[i]), _row(norm_ffn_post[i]), wup, wdown, tm=tm, tf=tf, ple=ple, casts=casts)

    tab_p = _rope_tables(jnp.arange(s, dtype=jnp.int32))
    tab_s = tuple(jnp.tile(t, (tm // ts, 1)) for t in _rope_tables(PAST_LEN + jnp.arange(ts, dtype=jnp.int32)))
    wqkv, bqkv = bf(w_qkv[0]), _row(b_qkv[0])
    g_pre, g_post = _row(norm_mix_pre[0]), _row(norm_mix_post[0])

    casts0 = ((w_ffn_up, 0), (w_ffn_down, 0), (w_ple_gate, 0), (w_ple_gate, 1), (w_o, 0))
    q_p, k_p, v_p, *rest = _qkv_rope(xp, g_pre, wqkv, bqkv, tab_p, tm=tm, q_dim=q_dim, kv_dim=kv_dim, casts=casts0)
    kv_p, (wup0, wdown0, wgate0, wgate1, wo) = rest[:4], rest[4:]
    q_s, k_s, v_s, *kv_s = _qkv_rope(xs, g_pre, wqkv, bqkv, tab_s, tm=tm, q_dim=q_dim, kv_dim=kv_dim)

    k_p3, v_p3 = k_p.reshape(b, s, kv_dim), v_p.reshape(b, s, kv_dim)
    k_s3, v_s3 = k_s.reshape(bs, ts, kv_dim), v_s.reshape(bs, ts, kv_dim)
    o_p = _attention(q_p.reshape(b, s, q_dim), *(t.reshape(b, s, -1) for t in kv_p), sinks[0], tq=TILES.attn_rows)
    cache = (*_pair_layouts(cache_k[0].reshape(bs, WINDOW, kv_dim), n_kv),
             *_pair_layouts(cache_v[0].reshape(bs, WINDOW, kv_dim), n_kv))
    kv_s = (jnp.concatenate([c, t.reshape(bs, ts, -1)], axis=1) for c, t in zip(cache, kv_s))
    o_s = _attention(q_s.reshape(bs, ts, q_dim), *kv_s, sinks[0], tq=ts)

    xp = _out_proj(o_p.reshape(b * s, q_dim), xp, wo, g_post, tm=TILES.rows_wide)
    xs = _out_proj(o_s.reshape(bs * ts, q_dim), xs, wo, g_post, tm=tm)

    xp, wup1, wdown1 = ffn(xp, 0, wup0, wdown0, tf=TILES.dff, casts=((w_ffn_up, 1), (w_ffn_down, 1)))
    (xs,) = ffn(xs, 0, wup0, wdown0, tf=TILES.dff_ple, p=ps, wgate=wgate0)

    g_pre, g_post = _row(norm_mix_pre[1]), _row(norm_mix_post[1])
    wpool, pscale = bf(w_pool[0]), _row(pool_scale[0])
    xp3, tail_p = _ple_pool(xp.reshape(b, s, d), pp, wgate0, wproj, g_pre, g_post, wpool, pscale,
                            layer=0, tm=tm)
    hist = jnp.pad(state_pool[0], ((0, 0), (HALO - POOL_HIST, 0), (0, 0)))
    xs3, tail_s = _pool_mixer(xs.reshape(bs, ts, d), hist, g_pre, g_post, wpool, pscale, pos0=PAST_LEN)

    (xp,) = ffn(xp3.reshape(b * s, d), 1, wup1, wdown1, tf=TILES.dff)
    xp = _ple(xp, pp, wgate1, wproj, layer=1, tm=TILES.rows_wide)
    (xs,) = ffn(xs3.reshape(bs * ts, d), 1, wup1, wdown1, tf=TILES.dff_ple, p=ps, wgate=wgate1)

    n_heads_kv = (n_kv, HEAD_DIM)
    return (
        xp.reshape(b, s, d),
        xs.reshape(bs, ts, d),
        k_p3[:, s - WINDOW:].reshape(1, b, WINDOW, *n_heads_kv),
        v_p3[:, s - WINDOW:].reshape(1, b, WINDOW, *n_heads_kv),
        tail_p[:, -1, HALO - POOL_HIST:][None],
        k_s3.reshape(1, bs, ts, *n_heads_kv),
        v_s3.reshape(1, bs, ts, *n_heads_kv),
        tail_s[:, HALO - POOL_HIST:][None],
    )
```
